```python
import functools
import math
import jax
import jax.numpy as jnp
from jax import lax
import numpy as np

D_MODEL = 1024
BATCH = 1
SEQ = 16384
DEPTH = 1
DEC_BATCH = 16
DEC_SEQ = 16
PAST_LEN = 1024

CHUNK = 64
Q_BLOCK = 128
EPS = 1e-6
H_A = 16
HD_A = 64
W_A = H_A * HD_A
D_INNER = 2 * D_MODEL
SSD_HEADDIM = 64
SSD_HEADS = D_INNER // SSD_HEADDIM
SSD_STATE = 128
SSD_GROUPS = 4
SSD_CHUNK = CHUNK
CONV_W = 4
CONV_DIM = D_INNER + 2 * SSD_GROUPS * SSD_STATE
D_FF = 2816
N_MOD = 9
IN_SPLITS = (W_A, W_A, W_A, H_A, D_INNER, CONV_DIM, SSD_HEADS, D_MODEL, D_MODEL)
D_IN_PROJ = 3 * W_A + H_A + D_INNER + CONV_DIM + SSD_HEADS + 2 * D_MODEL

kernel_name = 'fox_ssd_macaron_stream_step'


def rmsnorm(x, g):
    xf = x.astype(jnp.float32)
    y = xf * lax.rsqrt(jnp.mean(xf * xf, axis=-1, keepdims=True) + EPS)
    return (y * g.astype(jnp.float32)).astype(x.dtype)


def modulate(h, shift, scale):
    return h * (1 + scale[:, None, :]) + shift[:, None, :]


def swiglu(h, w1, w3, w2):
    return (jax.nn.silu(h @ w1) * (h @ w3)) @ w2


def split_cols(x, sizes):
    out, start = [], 0
    for s in sizes:
        out.append(x[..., start:start + s])
        start += s
    return out


def fox_attend(q, k, v, f_q, f_k, q_pos, k_pos):
    s = jnp.einsum('bqhd,bkhd->bhqk', q, k).astype(jnp.float32) * (HD_A ** -0.5)
    s = s + jnp.swapaxes(f_q, 1, 2)[:, :, :, None] - jnp.swapaxes(f_k, 1, 2)[:, :, None, :]
    s = jnp.where(k_pos[None, :] <= q_pos[:, None], s, -jnp.inf)
    p = jax.nn.softmax(s, axis=-1)
    return jnp.einsum('bhqk,bkhd->bqhd', p.astype(v.dtype), v)


def fox_prompt(q, k, v, logf):
    bsz, seq = q.shape[:2]
    f_cum = jnp.cumsum(logf, axis=1)
    k_pos = jnp.arange(seq)

    def block(i):
        start = i * Q_BLOCK
        qb = lax.dynamic_slice_in_dim(q, start, Q_BLOCK, axis=1)
        fb = lax.dynamic_slice_in_dim(f_cum, start, Q_BLOCK, axis=1)
        return fox_attend(qb, k, v, fb, f_cum, start + jnp.arange(Q_BLOCK), k_pos)

    out = lax.map(block, jnp.arange(seq // Q_BLOCK))
    return jnp.moveaxis(out, 0, 1).reshape(bsz, seq, H_A, HD_A)


def fox_sample(q, k, v, logf, ck, cv, clogf):
    past, n = ck.shape[1], q.shape[1]
    f_past = jnp.cumsum(clogf.astype(jnp.float32), axis=1)
    f_new = f_past[:, -1:] + jnp.cumsum(logf, axis=1)
    kk = jnp.concatenate([ck.astype(k.dtype), k], axis=1)
    vv = jnp.concatenate([cv.astype(v.dtype), v], axis=1)
    f_all = jnp.concatenate([f_past, f_new], axis=1)
    return fox_attend(q, kk, vv, f_new, f_all, past + jnp.arange(n), jnp.arange(past + n))


def causal_conv(xbc, conv_state, w, b):
    seq = xbc.shape[1]
    full = jnp.concatenate([conv_state.astype(xbc.dtype), xbc], axis=1)
    out = b + full[:, 0:seq] * w[0]
    for i in range(1, CONV_W):
        out = out + full[:, i:i + seq] * w[i]
    return jax.nn.silu(out), full[:, -(CONV_W - 1):]


def ssd_scan(x, dt, a, b_in, c_in, h0):
    bsz, seq = x.shape[:2]
    q = min(SSD_CHUNK, seq)
    nc = seq // q
    hg = SSD_HEADS // SSD_GROUPS
    f32 = jnp.float32
    x = x.astype(f32).reshape(bsz, nc, q, SSD_GROUPS, hg, SSD_HEADDIM)
    dt = dt.astype(f32).reshape(bsz, nc, q, SSD_GROUPS, hg)
    b_in = b_in.astype(f32).reshape(bsz, nc, q, SSD_GROUPS, SSD_STATE)
    c_in = c_in.astype(f32).reshape(bsz, nc, q, SSD_GROUPS, SSD_STATE)
    acs = jnp.moveaxis(jnp.cumsum(dt * a.reshape(SSD_GROUPS, hg), axis=2), 2, -1)
    dt_t = jnp.moveaxis(dt, 2, -1)
    causal = jnp.tril(jnp.ones((q, q), dtype=bool))
    seg = acs[..., :, None] - acs[..., None, :]
    decay = jnp.exp(jnp.where(causal, seg, -jnp.inf))
    cb = jnp.einsum('bctgn,bcsgn->bcgts', c_in, b_in)
    w = cb[:, :, :, None] * decay * dt_t[..., None, :]
    y_diag = jnp.einsum('bcghts,bcsghp->bctghp', w, x)
    decay_end = jnp.exp(acs[..., -1:] - acs) * dt_t
    states = jnp.einsum('bcsgn,bcghs,bcsghp->bcghpn', b_in, decay_end, x)
    chunk_decay = jnp.exp(acs[..., -1])
    h_init = h0.astype(f32).reshape(bsz, SSD_GROUPS, hg, SSD_HEADDIM, SSD_STATE)

    def step(h, inp):
        s_c, d_c = inp
        return d_c[..., None, None] * h + s_c, h

    h_last, h_in = lax.scan(step, h_init, (jnp.moveaxis(states, 1, 0), jnp.moveaxis(chunk_decay, 1, 0)))
    h_in = jnp.moveaxis(h_in, 0, 1)
    y_off = jnp.einsum('bctgn,bcghpn,bcght->bctghp', c_in, h_in, jnp.exp(acs))
    y = (y_diag + y_off).reshape(bsz, seq, SSD_HEADS, SSD_HEADDIM)
    return y, h_last.reshape(bsz, SSD_HEADS, SSD_HEADDIM, SSD_STATE).astype(h0.dtype)


def ssd_branch(z, xbc, dt_raw, conv_state, ssm_state, prm):
    bsz, seq = z.shape[:2]
    f32 = jnp.float32
    xbc, new_conv = causal_conv(xbc, conv_state, prm['conv_w'], prm['conv_b'])
    xs, b_in, c_in = split_cols(xbc, (D_INNER, SSD_GROUPS * SSD_STATE, SSD_GROUPS * SSD_STATE))
    xs = xs.reshape(bsz, seq, SSD_HEADS, SSD_HEADDIM)
    dt = jax.nn.softplus(dt_raw.astype(f32) + prm['dt_bias'].astype(f32))
    a = -jnp.exp(prm['a_log'].astype(f32))
    y, new_ssm = ssd_scan(xs, dt, a,
                          b_in.reshape(bsz, seq, SSD_GROUPS, SSD_STATE),
                          c_in.reshape(bsz, seq, SSD_GROUPS, SSD_STATE), ssm_state)
    y = y + prm['d_skip'].astype(f32)[:, None] * xs.astype(f32)
    y = y.reshape(bsz, seq, D_INNER).astype(z.dtype)
    y = rmsnorm(y * jax.nn.silu(z), prm['g_ssd'])
    return y @ prm['w_s'], new_conv, new_ssm


def token_mixer(u, prm, fox_fn, conv_state, ssm_state):
    bsz, seq = u.shape[:2]
    q, k, v, f_logit, z, xbc, dt_raw, gate_a, gate_s = split_cols(u @ prm['w_in'], IN_SPLITS)
    q = q.reshape(bsz, seq, H_A, HD_A)
    k = k.reshape(bsz, seq, H_A, HD_A)
    v = v.reshape(bsz, seq, H_A, HD_A)
    logf = jax.nn.log_sigmoid(f_logit.astype(jnp.float32) + prm['b_f'].astype(jnp.float32))
    o_a = fox_fn(q, k, v, logf).reshape(bsz, seq, W_A) @ prm['w_a']
    o_s, new_conv, new_ssm = ssd_branch(z, xbc, dt_raw, conv_state, ssm_state, prm)
    merged = jax.nn.sigmoid(gate_a) * o_a + jax.nn.sigmoid(gate_s) * o_s
    return merged @ prm['w_out'], (k, v, logf, new_ssm, new_conv)


def trunk_layer(x, c, prm, fox_fn, conv_state, ssm_state):
    mod = jax.nn.silu(c) @ prm['w_ada'] + prm['b_ada']
    sh1, sc1, ga1, sh2, sc2, ga2, sh3, sc3, ga3 = jnp.split(mod, N_MOD, axis=-1)
    h = modulate(rmsnorm(x, prm['g_ffn1']), sh1, sc1)
    x = x + 0.5 * (1 + ga1)[:, None, :] * swiglu(h, prm['w1_ffn1'], prm['w3_ffn1'], prm['w2_ffn1'])
    u = modulate(rmsnorm(x, prm['g_mix']), sh2, sc2)
    m, new_state = token_mixer(u, prm, fox_fn, conv_state, ssm_state)
    x = x + (1 + ga2)[:, None, :] * m
    h = modulate(rmsnorm(x, prm['g_ffn2']), sh3, sc3)
    x = x + 0.5 * (1 + ga3)[:, None, :] * swiglu(h, prm['w1_ffn2'], prm['w3_ffn2'], prm['w2_ffn2'])
    return x, new_state


def setup_inputs(seed: int = 0) -> dict:
    key = jax.random.key(seed)
    ks = iter(jax.random.split(key, 48))
    L = DEPTH
    f32 = jnp.float32

    def nrm(shape, scale):
        return scale * jax.random.normal(next(ks), shape, f32)

    def unif(shape, lo, hi):
        return jax.random.uniform(next(ks), shape, f32, minval=lo, maxval=hi)

    dt0 = jnp.exp(unif((L, SSD_HEADS), math.log(1e-3), math.log(1e-1)))
    inputs = {
        'x_prompt': nrm((BATCH, SEQ, D_MODEL), 1.0),
        'x_sample': nrm((DEC_BATCH, DEC_SEQ, D_MODEL), 1.0),
        'c_prompt': nrm((BATCH, D_MODEL), 1.0),
        'c_sample': nrm((DEC_BATCH, D_MODEL), 1.0),
        'cache_k': nrm((L, DEC_BATCH, PAST_LEN, H_A, HD_A), 1.0),
        'cache_v': nrm((L, DEC_BATCH, PAST_LEN, H_A, HD_A), 1.0),
        'cache_logf': jax.nn.log_sigmoid(unif((L, DEC_BATCH, PAST_LEN, H_A), 1.0, 6.0)
                                         + nrm((L, DEC_BATCH, PAST_LEN, H_A), 1.0)),
        'state_ssm': nrm((L, DEC_BATCH, SSD_HEADS, SSD_HEADDIM, SSD_STATE), 0.5),
        'state_conv': nrm((L, DEC_BATCH, CONV_W - 1, CONV_DIM), 1.0),
        'w_ada': nrm((L, D_MODEL, N_MOD * D_MODEL), 0.3 * D_MODEL ** -0.5),
        'b_ada': nrm((L, N_MOD * D_MODEL), 0.01),
        'g_ffn1': 1.0 + nrm((L, D_MODEL), 0.01),
        'w1_ffn1': nrm((L, D_MODEL, D_FF), D_MODEL ** -0.5),
        'w3_ffn1': nrm((L, D_MODEL, D_FF), D_MODEL ** -0.5),
        'w2_ffn1': nrm((L, D_FF, D_MODEL), D_FF ** -0.5),
        'g_mix': 1.0 + nrm((L, D_MODEL), 0.01),
        'w_in': nrm((L, D_MODEL, D_IN_PROJ), D_MODEL ** -0.5),
        'b_f': unif((L, H_A), 1.0, 6.0),
        'conv_w': nrm((L, CONV_W, CONV_DIM), CONV_W ** -0.5),
        'conv_b': nrm((L, CONV_DIM), 0.01),
        'dt_bias': dt0 + jnp.log(-jnp.expm1(-dt0)),
        'a_log': jnp.log(unif((L, SSD_HEADS), 1.0, 16.0)),
        'd_skip': 1.0 + nrm((L, SSD_HEADS), 0.01),
        'g_ssd': 1.0 + nrm((L, D_INNER), 0.01),
        'w_a': nrm((L, W_A, D_MODEL), W_A ** -0.5),
        'w_s': nrm((L, D_INNER, D_MODEL), D_INNER ** -0.5),
        'w_out': nrm((L, D_MODEL, D_MODEL), D_MODEL ** -0.5),
        'g_ffn2': 1.0 + nrm((L, D_MODEL), 0.01),
        'w1_ffn2': nrm((L, D_MODEL, D_FF), D_MODEL ** -0.5),
        'w3_ffn2': nrm((L, D_MODEL, D_FF), D_MODEL ** -0.5),
        'w2_ffn2': nrm((L, D_FF, D_MODEL), D_FF ** -0.5),
        'g_final': 1.0 + nrm((D_MODEL,), 0.01),
    }
    return inputs


def reference(x_prompt, x_sample, c_prompt, c_sample, cache_k, cache_v, cache_logf, state_ssm, state_conv,
              w_ada, b_ada, g_ffn1, w1_ffn1, w3_ffn1, w2_ffn1, g_mix, w_in, b_f, conv_w, conv_b,
              dt_bias, a_log, d_skip, g_ssd, w_a, w_s, w_out, g_ffn2, w1_ffn2, w3_ffn2, w2_ffn2, g_final):
    bp = x_prompt.shape[0]
    xp, xs = x_prompt, x_sample
    kp_l, vp_l, lp_l, sp_l, cp_l = [], [], [], [], []
    ks_l, vs_l, ls_l, ss_l, cs_l = [], [], [], [], []
    for l in range(DEPTH):
        prm = {
            'w_ada': w_ada[l], 'b_ada': b_ada[l],
            'g_ffn1': g_ffn1[l], 'w1_ffn1': w1_ffn1[l], 'w3_ffn1': w3_ffn1[l], 'w2_ffn1': w2_ffn1[l],
            'g_mix': g_mix[l], 'w_in': w_in[l], 'b_f': b_f[l],
            'conv_w': conv_w[l], 'conv_b': conv_b[l], 'dt_bias': dt_bias[l], 'a_log': a_log[l],
            'd_skip': d_skip[l], 'g_ssd': g_ssd[l], 'w_a': w_a[l], 'w_s': w_s[l], 'w_out': w_out[l],
            'g_ffn2': g_ffn2[l], 'w1_ffn2': w1_ffn2[l], 'w3_ffn2': w3_ffn2[l], 'w2_ffn2': w2_ffn2[l],
        }
        zero_conv = jnp.zeros((bp, CONV_W - 1, CONV_DIM), x_prompt.dtype)
        zero_ssm = jnp.zeros((bp, SSD_HEADS, SSD_HEADDIM, SSD_STATE), jnp.float32)
        xp, (kp, vp, lp, sp, cp) = trunk_layer(xp, c_prompt, prm, fox_prompt, zero_conv, zero_ssm)
        fox_s = functools.partial(fox_sample, ck=cache_k[l], cv=cache_v[l], clogf=cache_logf[l])
        xs, (ks_, vs_, ls_, ss_, cs_) = trunk_layer(xs, c_sample, prm, fox_s, state_conv[l], state_ssm[l])
        kp_l.append(kp); vp_l.append(vp); lp_l.append(lp); sp_l.append(sp); cp_l.append(cp)
        ks_l.append(ks_); vs_l.append(vs_); ls_l.append(ls_); ss_l.append(ss_); cs_l.append(cs_)
    y_prompt = rmsnorm(xp, g_final)
    y_sample = rmsnorm(xs, g_final)
    k_prompt = jnp.stack(kp_l)
    v_prompt = jnp.stack(vp_l)
    logf_prompt = jnp.stack(lp_l)
    ssm_prompt = jnp.stack(sp_l)
    conv_prompt = jnp.stack(cp_l)
    k_sample = jnp.stack(ks_l)
    v_sample = jnp.stack(vs_l)
    logf_sample = jnp.stack(ls_l)
    ssm_sample = jnp.stack(ss_l)
    conv_sample = jnp.stack(cs_l)
    return (y_prompt, y_sample, k_prompt, v_prompt, logf_prompt, ssm_prompt, conv_prompt,
            k_sample, v_sample, logf_sample, ssm_sample, conv_sample)
```

```python
import functools

import jax
import jax.numpy as jnp
from jax import lax
from jax.experimental import pallas as pl
from jax.experimental.pallas import tpu as pltpu

F32 = jnp.float32
BF16 = jnp.bfloat16

EPS = 1e-6
D_MODEL = 1024
N_HEADS_A = 16
HEAD_DIM_A = 64
W_ATT = N_HEADS_A * HEAD_DIM_A
D_INNER = 2 * D_MODEL
SSD_HEADDIM = 64
SSD_HEADS = D_INNER // SSD_HEADDIM
SSD_STATE = 128
SSD_GROUPS = 4
CONV_W = 4
CONV_DIM = D_INNER + 2 * SSD_GROUPS * SSD_STATE
D_FF = 2816
N_MOD = 9

LANES = 128
VMEM_LIMIT_BYTES = 56 * 1024 * 1024

ROW_TILE = 512
FF_CHUNK = 1408
ATT_BLOCK = 512
SSD_CHUNK_PROMPT = 128
CONV_PAD = 8

_NT = (((1,), (1,)), ((), ()))
_TN = (((0,), (0,)), ((), ()))


def _resident(shape):
    nd = len(shape)
    return pl.BlockSpec(shape, lambda *_: (0,) * nd, pipeline_mode=pl.Buffered(1))


def _params(n_axes):
    return pltpu.CompilerParams(dimension_semantics=("arbitrary",) * n_axes,
                                vmem_limit_bytes=VMEM_LIMIT_BYTES)


def _silu(x):
    return x * jax.nn.sigmoid(x)


def _softplus(x):
    return jnp.maximum(x, 0.0) + jnp.log1p(jnp.exp(-jnp.abs(x)))


def _split3(x):
    x1 = x.astype(BF16)
    r1 = x - x1.astype(F32)
    x2 = r1.astype(BF16)
    r2 = r1 - x2.astype(F32)
    return x1, x2, r2.astype(BF16)


def _dot01_right(x, u01):
    acc = None
    for piece in _split3(x):
        t = jnp.dot(piece, u01, preferred_element_type=F32)
        acc = t if acc is None else acc + t
    return acc


def _dot01_left(l01, x):
    acc = None
    for piece in _split3(x):
        t = jnp.dot(l01, piece, preferred_element_type=F32)
        acc = t if acc is None else acc + t
    return acc


def _transpose01(eye, x):
    acc = None
    for piece in _split3(x):
        t = lax.dot_general(eye, piece, _NT, preferred_element_type=F32)
        acc = t if acc is None else acc + t
    return acc


def _iota(shape, dim):
    return lax.broadcasted_iota(jnp.int32, shape, dim)


def _modulated_norm(x, g, shift, scale):
    ms = jnp.mean(x * x, axis=-1, keepdims=True)
    h = x * lax.rsqrt(ms + EPS) * g
    return h * (1.0 + scale) + shift


def _ada_kernel(c_ref, w_ref, b_ref, o_ref):
    c = _silu(c_ref[...]).astype(BF16)
    o_ref[...] = jnp.dot(c, w_ref[...].astype(BF16), preferred_element_type=F32) + b_ref[...]


def _ada(c_all, w_ada, b_ada):
    rows = c_all.shape[0]
    n = w_ada.shape[1]
    tn = 1536
    return pl.pallas_call(
        _ada_kernel,
        grid=(n // tn,),
        in_specs=[pl.BlockSpec((rows, D_MODEL), lambda j: (0, 0)),
                  pl.BlockSpec((D_MODEL, tn), lambda j: (0, j)),
                  pl.BlockSpec((1, tn), lambda j: (0, j))],
        out_specs=pl.BlockSpec((rows, tn), lambda j: (0, j)),
        out_shape=jax.ShapeDtypeStruct((rows, n), F32),
        compiler_params=_params(1),
        name="ada_mod",
    )(c_all, w_ada, b_ada.reshape(1, n))


def _mod_spec(mod, tm, k):
    if mod.shape[0] == 1:
        return pl.BlockSpec((1, D_MODEL), lambda i: (0, k))
    return pl.BlockSpec((tm, D_MODEL), lambda i: (i, k))


def _row_spec(tm, width):
    return pl.BlockSpec((tm, width), lambda i: (i, 0))


def _ffn_kernel(x_ref, sh_ref, sc_ref, ga_ref, g_ref, w1_ref, w3_ref, w2_ref, gf_ref, o_ref, *maybe_y_ref,
                final_norm):
    x = x_ref[...]
    h = _modulated_norm(x, g_ref[...], sh_ref[...], sc_ref[...]).astype(BF16)
    y = None
    for c0 in range(0, D_FF, FF_CHUNK):
        a = jnp.dot(h, w1_ref[:, c0:c0 + FF_CHUNK], preferred_element_type=F32)
        b = jnp.dot(h, w3_ref[:, c0:c0 + FF_CHUNK], preferred_element_type=F32)
        t = jnp.dot((_silu(a) * b).astype(BF16), w2_ref[c0:c0 + FF_CHUNK, :], preferred_element_type=F32)
        y = t if y is None else y + t
    out = x + 0.5 * (1.0 + ga_ref[...]) * y
    o_ref[...] = out
    if final_norm:
        (y_ref,) = maybe_y_ref
        ms = jnp.mean(out * out, axis=-1, keepdims=True)
        y_ref[...] = out * lax.rsqrt(ms + EPS) * gf_ref[...]


def _ffn(x, mod, k0, g, w1, w3, w2, g_final, tm, final_norm):
    rows = x.shape[0]
    n_out = 2 if final_norm else 1
    res = pl.pallas_call(
        functools.partial(_ffn_kernel, final_norm=final_norm),
        grid=(rows // tm,),
        in_specs=[_row_spec(tm, D_MODEL),
                  _mod_spec(mod, tm, k0), _mod_spec(mod, tm, k0 + 1), _mod_spec(mod, tm, k0 + 2),
                  _resident((1, D_MODEL)),
                  _resident((D_MODEL, D_FF)), _resident((D_MODEL, D_FF)), _resident((D_FF, D_MODEL)),
                  _resident((1, D_MODEL))],
        out_specs=[_row_spec(tm, D_MODEL)] * n_out,
        out_shape=[jax.ShapeDtypeStruct((rows, D_MODEL), F32)] * n_out,
        compiler_params=_params(1),
        name="ffn_final" if final_norm else "ffn",
    )(x, mod, mod, mod, g, w1, w3, w2, g_final)
    return (res[0], res[1]) if final_norm else (res[0], None)


def _inproj_a_kernel(x_ref, sh_ref, sc_ref, g_ref, wq_ref, wk_ref, wv_ref, ws_ref, bs_ref,
                     q_ref, k32_ref, v32_ref, kb_ref, vb_ref, lf_ref, dt_ref):
    u = _modulated_norm(x_ref[...], g_ref[...], sh_ref[...], sc_ref[...]).astype(BF16)
    q = jnp.dot(u, wq_ref[...], preferred_element_type=F32)
    q_ref[...] = (q * (HEAD_DIM_A ** -0.5)).astype(BF16)
    k = jnp.dot(u, wk_ref[...], preferred_element_type=F32)
    k32_ref[...] = k
    kb_ref[...] = k.astype(BF16)
    v = jnp.dot(u, wv_ref[...], preferred_element_type=F32)
    v32_ref[...] = v
    vb_ref[...] = v.astype(BF16)
    sm = jnp.dot(u, ws_ref[...], preferred_element_type=F32) + bs_ref[...]
    logf = -_softplus(-sm[:, :LANES])
    lf_ref[...] = logf[:, :N_HEADS_A]
    lane = _iota((1, LANES), 1)
    dt_ref[...] = jnp.where(lane < SSD_HEADS, _softplus(sm[:, LANES:]), 0.0)


def _inproj_a(x, mod, g, wq, wk, wv, ws, bs, tm):
    rows = x.shape[0]
    sd = jax.ShapeDtypeStruct
    return pl.pallas_call(
        _inproj_a_kernel,
        grid=(rows // tm,),
        in_specs=[_row_spec(tm, D_MODEL), _mod_spec(mod, tm, 3), _mod_spec(mod, tm, 4),
                  _resident((1, D_MODEL)),
                  _resident((D_MODEL, W_ATT)), _resident((D_MODEL, W_ATT)), _resident((D_MODEL, W_ATT)),
                  _resident((D_MODEL, 2 * LANES)), _resident((1, 2 * LANES))],
        out_specs=[_row_spec(tm, W_ATT)] * 5 + [_row_spec(tm, N_HEADS_A), _row_spec(tm, LANES)],
        out_shape=[sd((rows, W_ATT), BF16), sd((rows, W_ATT), F32), sd((rows, W_ATT), F32),
                   sd((rows, W_ATT), BF16), sd((rows, W_ATT), BF16),
                   sd((rows, N_HEADS_A), F32), sd((rows, LANES), F32)],
        compiler_params=_params(1),
        name="inproj_qkv",
    )(x, mod, mod, g, wq, wk, wv, ws, bs)


def _inproj_b_kernel(x_ref, sh_ref, sc_ref, g_ref, wz_ref, wx_ref, wga_ref, wgs_ref,
                     z_ref, xbc_ref, ga_ref, gs_ref):
    u = _modulated_norm(x_ref[...], g_ref[...], sh_ref[...], sc_ref[...]).astype(BF16)
    z_ref[...] = jnp.dot(u, wz_ref[...], preferred_element_type=F32).astype(BF16)
    xbc_ref[...] = jnp.dot(u, wx_ref[...], preferred_element_type=F32)
    ga_ref[...] = jnp.dot(u, wga_ref[...], preferred_element_type=F32).astype(BF16)
    gs_ref[...] = jnp.dot(u, wgs_ref[...], preferred_element_type=F32).astype(BF16)


def _inproj_b(x, mod, g, wz, wx, wga, wgs, tm):
    rows = x.shape[0]
    sd = jax.ShapeDtypeStruct
    return pl.pallas_call(
        _inproj_b_kernel,
        grid=(rows // tm,),
        in_specs=[_row_spec(tm, D_MODEL), _mod_spec(mod, tm, 3), _mod_spec(mod, tm, 4),
                  _resident((1, D_MODEL)),
                  _resident((D_MODEL, D_INNER)), _resident((D_MODEL, CONV_DIM)),
                  _resident((D_MODEL, D_MODEL)), _resident((D_MODEL, D_MODEL))],
        out_specs=[_row_spec(tm, D_INNER), _row_spec(tm, CONV_DIM),
                   _row_spec(tm, D_MODEL), _row_spec(tm, D_MODEL)],
        out_shape=[sd((rows, D_INNER), BF16), sd((rows, CONV_DIM), F32),
                   sd((rows, D_MODEL), BF16), sd((rows, D_MODEL), BF16)],
        compiler_params=_params(1),
        name="inproj_ssd",
    )(x, mod, mod, g, wz, wx, wga, wgs)


def _merge_kernel(x_ref, o_ref, y_ref, z_ref, ga_ref, gs_ref, g2_ref, gssd_ref, wa_ref, ws_ref, wo_ref,
                  out_ref):
    yg = y_ref[...].astype(F32) * _silu(z_ref[...].astype(F32))
    ms = jnp.mean(yg * yg, axis=-1, keepdims=True)
    yn = (yg * lax.rsqrt(ms + EPS) * gssd_ref[...]).astype(BF16)
    o_s = jnp.dot(yn, ws_ref[...], preferred_element_type=F32)
    o_a = jnp.dot(o_ref[...], wa_ref[...], preferred_element_type=F32)
    merged = (jax.nn.sigmoid(ga_ref[...].astype(F32)) * o_a
              + jax.nn.sigmoid(gs_ref[...].astype(F32)) * o_s)
    m = jnp.dot(merged.astype(BF16), wo_ref[...], preferred_element_type=F32)
    out_ref[...] = x_ref[...] + (1.0 + g2_ref[...]) * m


def _merge(x, o, y, z, ga, gs, mod, g_ssd, wa, ws, wo, tm):
    rows = x.shape[0]
    return pl.pallas_call(
        _merge_kernel,
        grid=(rows // tm,),
        in_specs=[_row_spec(tm, D_MODEL), _row_spec(tm, W_ATT), _row_spec(tm, D_INNER),
                  _row_spec(tm, D_INNER), _row_spec(tm, D_MODEL), _row_spec(tm, D_MODEL),
                  _mod_spec(mod, tm, 5), _resident((1, D_INNER)),
                  _resident((W_ATT, D_MODEL)), _resident((D_INNER, D_MODEL)), _resident((D_MODEL, D_MODEL))],
        out_specs=_row_spec(tm, D_MODEL),
        out_shape=jax.ShapeDtypeStruct((rows, D_MODEL), F32),
        compiler_params=_params(1),
        name="merge",
    )(x, o, y, z, ga, gs, mod, g_ssd, wa, ws, wo)


def _fcum_prompt_kernel(x_ref, o_ref):
    r = _iota((LANES, LANES), 0)
    c = _iota((LANES, LANES), 1)
    upper = (r <= c).astype(BF16)
    strict_lower = (c < r).astype(BF16)
    ones = jnp.ones((LANES, LANES), BF16)
    x = x_ref[...]
    within = _dot01_right(x, upper)
    row_tot = _dot01_right(x, ones)
    o_ref[...] = within + _dot01_left(strict_lower, row_tot)


def _fcum_prompt(lf_heads):
    h, s = lf_heads.shape
    assert s == LANES * LANES
    out = pl.pallas_call(
        _fcum_prompt_kernel,
        grid=(h,),
        in_specs=[pl.BlockSpec((None, LANES, LANES), lambda i: (i, 0, 0))],
        out_specs=pl.BlockSpec((None, LANES, LANES), lambda i: (i, 0, 0)),
        out_shape=jax.ShapeDtypeStruct((h, LANES, LANES), F32),
        compiler_params=_params(1),
        name="fcum_prompt",
    )(lf_heads.reshape(h, LANES, LANES))
    return out.reshape(h, s)


def _fcum_sample_kernel(past_ref, new_ref, fp_ref, fn_ref):
    p = past_ref.shape[1]
    upper_p = (_iota((p, p), 0) <= _iota((p, p), 1)).astype(BF16)
    upper_n = (_iota((LANES, LANES), 0) <= _iota((LANES, LANES), 1)).astype(BF16)
    past = past_ref[...]
    fp_ref[...] = _dot01_right(past, upper_p)
    total = _dot01_right(past, jnp.ones((p, LANES), BF16))
    fn_ref[...] = total + _dot01_right(new_ref[...], upper_n)


def _fcum_sample(past, new):
    rows, p = past.shape
    return pl.pallas_call(
        _fcum_sample_kernel,
        out_shape=[jax.ShapeDtypeStruct((rows, p), F32), jax.ShapeDtypeStruct((rows, LANES), F32)],
        compiler_params=pltpu.CompilerParams(vmem_limit_bytes=VMEM_LIMIT_BYTES),
        name="fcum_sample",
    )(past, new)


def _pair_split(x):
    lane = _iota((1, LANES), 1)
    keep0 = (lane < HEAD_DIM_A).astype(F32).astype(x.dtype)
    keep1 = (lane >= HEAD_DIM_A).astype(F32).astype(x.dtype)
    return x * keep0, x * keep1


def _pair_value_operands(v):
    lane = _iota((1, LANES), 1)
    v0, v1 = _pair_split(v)
    ones0 = (lane == HEAD_DIM_A).astype(F32).astype(v.dtype)
    ones1 = (lane == 0).astype(F32).astype(v.dtype)
    return v0 + ones0, v1 + ones1


def _attn_prompt_kernel(q_ref, k_ref, v_ref, f_ref, o_ref, qm_sc, m_sc, acc_sc):
    qi = pl.program_id(1)
    tq = q_ref.shape[0]
    tk = tq
    lane = _iota((1, LANES), 1)
    qm_sc[0], qm_sc[1] = _pair_split(q_ref[...])
    m_sc[...] = jnp.full(m_sc.shape, -jnp.inf, F32)
    acc_sc[...] = jnp.zeros(acc_sc.shape, F32)

    def block(j, diagonal):
        off = pl.multiple_of(j * tk, tk)
        k = k_ref[pl.ds(off, tk), :]
        v_ops = _pair_value_operands(v_ref[pl.ds(off, tk), :])
        fk = f_ref[j]
        for hh in range(2):
            s = lax.dot_general(qm_sc[hh], k, _NT, preferred_element_type=F32) - fk[hh:hh + 1, :]
            if diagonal:
                keep = _iota((tq, tk), 1) <= _iota((tq, tk), 0)
                s = jnp.where(keep, s, -jnp.inf)
            m_prev = m_sc[hh][:, :1]
            m_new = jnp.maximum(m_prev, jnp.max(s, axis=1, keepdims=True))
            p = jnp.exp(s - m_new)
            alpha = jnp.exp(m_prev - m_new)
            pv = jnp.dot(p.astype(BF16), v_ops[hh], preferred_element_type=F32)
            acc_sc[hh] = alpha * acc_sc[hh] + pv
            m_sc[hh] = jnp.broadcast_to(m_new, (tq, LANES))

    def body(j, carry):
        block(j, False)
        return carry

    lax.fori_loop(0, qi, body, 0)
    block(qi, True)

    a0 = acc_sc[0]
    a1 = acc_sc[1]
    out = jnp.where(lane < HEAD_DIM_A, a0 / a0[:, HEAD_DIM_A:HEAD_DIM_A + 1], a1 / a1[:, 0:1])
    o_ref[...] = out.astype(o_ref.dtype)


def _attn_prompt(q, k, v, f_blocks):
    s = q.shape[0]
    tq = ATT_BLOCK
    n_pairs = W_ATT // LANES
    return pl.pallas_call(
        _attn_prompt_kernel,
        grid=(n_pairs, s // tq),
        in_specs=[pl.BlockSpec((tq, LANES), lambda p, i: (i, p)),
                  pl.BlockSpec((s, LANES), lambda p, i: (0, p)),
                  pl.BlockSpec((s, LANES), lambda p, i: (0, p)),
                  pl.BlockSpec((None, s // tq, 8, tq), lambda p, i: (p, 0, 0, 0))],
        out_specs=pl.BlockSpec((tq, LANES), lambda p, i: (i, p)),
        out_shape=jax.ShapeDtypeStruct((s, W_ATT), BF16),
        scratch_shapes=[pltpu.VMEM((2, tq, LANES), BF16),
                        pltpu.VMEM((2, tq, LANES), F32),
                        pltpu.VMEM((2, tq, LANES), F32)],
        compiler_params=_params(2),
        name="attn_prompt",
    )(q, k, v, f_blocks)


def _attn_sample_kernel(q_ref, kn_ref, vn_ref, kc_ref, vc_ref, fp_ref, fn_ref, o_ref):
    n = q_ref.shape[0]
    lane = _iota((1, LANES), 1)
    qm = _pair_split(q_ref[...])
    kc = kc_ref[...].astype(BF16)
    kn = kn_ref[...]
    vc_ops = _pair_value_operands(vc_ref[...].astype(BF16))
    vn_ops = _pair_value_operands(vn_ref[...])
    fp = fp_ref[...]
    fn = fn_ref[...][:, :n]
    keep = _iota((n, n), 1) <= _iota((n, n), 0)
    accs = []
    for hh in range(2):
        s_past = lax.dot_general(qm[hh], kc, _NT, preferred_element_type=F32) - fp[hh:hh + 1, :]
        s_new = lax.dot_general(qm[hh], kn, _NT, preferred_element_type=F32) - fn[hh:hh + 1, :]
        s_new = jnp.where(keep, s_new, -jnp.inf)
        m = jnp.maximum(jnp.max(s_past, axis=1, keepdims=True), jnp.max(s_new, axis=1, keepdims=True))
        p_past = jnp.exp(s_past - m).astype(BF16)
        p_new = jnp.exp(s_new - m).astype(BF16)
        accs.append(jnp.dot(p_past, vc_ops[hh], preferred_element_type=F32)
                    + jnp.dot(p_new, vn_ops[hh], preferred_element_type=F32))
    a0, a1 = accs
    out = jnp.where(lane < HEAD_DIM_A, a0 / a0[:, HEAD_DIM_A:HEAD_DIM_A + 1], a1 / a1[:, 0:1])
    o_ref[...] = out.astype(o_ref.dtype)


def _attn_sample(q, kn, vn, cache_k, cache_v, f_past, f_new, n_batch, n_new):
    past = cache_k.shape[1]
    n_pairs = W_ATT // LANES
    tok = pl.BlockSpec((n_new, LANES), lambda b, p: (b, p))
    cache = pl.BlockSpec((None, past, LANES), lambda b, p: (b, 0, p))
    return pl.pallas_call(
        _attn_sample_kernel,
        grid=(n_batch, n_pairs),
        in_specs=[tok, tok, tok, cache, cache,
                  pl.BlockSpec((None, None, 2, past), lambda b, p: (b, p, 0, 0)),
                  pl.BlockSpec((None, None, 2, LANES), lambda b, p: (b, p, 0, 0))],
        out_specs=tok,
        out_shape=jax.ShapeDtypeStruct((n_batch * n_new, W_ATT), BF16),
        compiler_params=_params(2),
        name="attn_sample",
    )(q, kn, vn, cache_k, cache_v, f_past, f_new)


def _ssd_kernel(xbc_ref, dt_ref, cst_ref, h0_ref, cw_ref, cb_ref, alog_ref, dsk_ref,
                y_ref, hout_ref, cout_ref, ht_sc, cbuf_sc):
    c = pl.program_id(1)
    n_chunks = pl.num_programs(1)
    t = xbc_ref.shape[0]
    hp_group = D_INNER // SSD_GROUPS
    lane = _iota((1, LANES), 1)
    first_tail_row = CONV_PAD - (CONV_W - 1)

    @pl.when(c == 0)
    def _():
        cbuf_sc[first_tail_row:CONV_PAD, :] = cst_ref[...]
        for b in range(D_INNER // LANES):
            ht_sc[:, b * LANES:(b + 1) * LANES] = h0_ref[b * LANES:(b + 1) * LANES, :].T

    cbuf_sc[CONV_PAD:CONV_PAD + t, :] = xbc_ref[...]
    conv = cb_ref[...] + cw_ref[0:1, :] * cbuf_sc[first_tail_row:first_tail_row + t, :]
    for i in range(1, CONV_W):
        conv = conv + cw_ref[i:i + 1, :] * cbuf_sc[first_tail_row + i:first_tail_row + i + t, :]
    xc = _silu(conv)
    tail = cbuf_sc[t + first_tail_row:t + CONV_PAD, :]
    cbuf_sc[first_tail_row:CONV_PAD, :] = tail

    @pl.when(c == n_chunks - 1)
    def _():
        cout_ref[...] = tail

    xs = xc[:, :D_INNER]
    b_all = xc[:, D_INNER:D_INNER + SSD_GROUPS * SSD_STATE].astype(BF16)
    c_all = xc[:, D_INNER + SSD_GROUPS * SSD_STATE:].astype(BF16)

    dt = dt_ref[...]
    a_row = -jnp.exp(alog_ref[...])
    lower = (_iota((t, t), 1) <= _iota((t, t), 0)).astype(BF16)
    acs = _dot01_left(lower, dt * a_row)
    acs_end = acs[t - 1:t, :]
    compact = jnp.concatenate([jnp.exp(acs_end - acs) * dt, dt, jnp.exp(acs)], axis=0)
    expand = ((_iota((LANES, D_INNER), 1) // SSD_HEADDIM) == _iota((LANES, D_INNER), 0)).astype(BF16)
    wide = _dot01_right(compact, expand)
    xs_state = (xs * wide[0:t]).astype(BF16)
    xs_dt = (xs * wide[t:2 * t]).astype(BF16)
    e_acs = wide[2 * t:3 * t]
    chunk_decay = e_acs[t - 1:t, :]

    eye = (_iota((LANES, LANES), 0) == _iota((LANES, LANES), 1)).astype(BF16)
    acs_t = _transpose01(eye, acs)
    causal = _iota((t, t), 1) <= _iota((t, t), 0)

    for g in range(SSD_GROUPS):
        b_g = b_all[:, g * SSD_STATE:(g + 1) * SSD_STATE]
        c_g = c_all[:, g * SSD_STATE:(g + 1) * SSD_STATE]
        cols = slice(g * hp_group, (g + 1) * hp_group)
        ht_g = ht_sc[:, cols]
        y_off = jnp.dot(c_g, ht_g.astype(BF16), preferred_element_type=F32) * e_acs[:, cols]
        states = lax.dot_general(b_g, xs_state[:, cols], _TN, preferred_element_type=F32)
        ht_sc[:, cols] = ht_g * chunk_decay[:, cols] + states
        cb = lax.dot_general(c_g, b_g, _NT, preferred_element_type=F32)
        heads_per_group = SSD_HEADS // SSD_GROUPS
        for pair in range(heads_per_group // 2):
            h0 = g * heads_per_group + 2 * pair
            lo = h0 * SSD_HEADDIM
            x_ops = _pair_split(xs_dt[:, lo:lo + LANES])
            y_pair = y_off[:, lo - g * hp_group:lo - g * hp_group + LANES]
            for hh in range(2):
                h = h0 + hh
                seg = acs[:, h:h + 1] - acs_t[h:h + 1, :]
                w = (cb * jnp.exp(jnp.where(causal, seg, -jnp.inf))).astype(BF16)
                y_pair = y_pair + jnp.dot(w, x_ops[hh], preferred_element_type=F32)
            y_pair = y_pair + dsk_ref[:, lo:lo + LANES] * xs[:, lo:lo + LANES]
            y_ref[:, lo:lo + LANES] = y_pair.astype(y_ref.dtype)

    @pl.when(c == n_chunks - 1)
    def _():
        for b in range(D_INNER // LANES):
            hout_ref[b * LANES:(b + 1) * LANES, :] = ht_sc[:, b * LANES:(b + 1) * LANES].T


def _ssd(xbc, dt, conv_state, ssm_state, conv_w, conv_b, a_log_row, d_skip_row, n_batch, seq, chunk):
    assert seq % chunk == 0 and chunk >= CONV_W - 1
    nc = seq // chunk
    rows = lambda b, c: (b * nc + c, 0)
    per_batch3 = lambda b, c: (b, 0, 0)
    sd = jax.ShapeDtypeStruct
    return pl.pallas_call(
        _ssd_kernel,
        grid=(n_batch, nc),
        in_specs=[pl.BlockSpec((chunk, CONV_DIM), rows),
                  pl.BlockSpec((chunk, LANES), rows),
                  pl.BlockSpec((None, CONV_W - 1, CONV_DIM), per_batch3),
                  pl.BlockSpec((None, D_INNER, SSD_STATE), per_batch3),
                  pl.BlockSpec((CONV_W, CONV_DIM), lambda b, c: (0, 0)),
                  pl.BlockSpec((1, CONV_DIM), lambda b, c: (0, 0)),
                  pl.BlockSpec((1, LANES), lambda b, c: (0, 0)),
                  pl.BlockSpec((1, D_INNER), lambda b, c: (0, 0))],
        out_specs=[pl.BlockSpec((chunk, D_INNER), rows),
                   pl.BlockSpec((None, D_INNER, SSD_STATE), per_batch3),
                   pl.BlockSpec((None, CONV_W - 1, CONV_DIM), per_batch3)],
        out_shape=[sd((n_batch * seq, D_INNER), BF16),
                   sd((n_batch, D_INNER, SSD_STATE), F32),
                   sd((n_batch, CONV_W - 1, CONV_DIM), F32)],
        scratch_shapes=[pltpu.VMEM((SSD_STATE, D_INNER), F32),
                        pltpu.VMEM((CONV_PAD + chunk, CONV_DIM), F32)],
        compiler_params=_params(2),
        name="ssd_scan",
    )(xbc, dt, conv_state, ssm_state, conv_w, conv_b, a_log_row, d_skip_row)


def _pad_lanes(row, width=LANES):
    return jnp.pad(row, ((0, 0), (0, width - row.shape[1])))


def _layer_weights(w_in, b_f, dt_bias, a_log, d_skip, prm):
    o = 0
    pieces = {}
    for name, width in (("q", W_ATT), ("k", W_ATT), ("v", W_ATT), ("f", N_HEADS_A), ("z", D_INNER),
                        ("xbc", CONV_DIM), ("dt", SSD_HEADS), ("ga", D_MODEL), ("gs", D_MODEL)):
        pieces[name] = w_in[:, o:o + width]
        o += width
    w = {k: pieces[k].astype(BF16) for k in ("q", "k", "v", "z", "xbc", "ga", "gs")}
    w["small"] = jnp.concatenate([_pad_lanes(pieces["f"]), _pad_lanes(pieces["dt"])], axis=1).astype(BF16)
    w["small_bias"] = jnp.concatenate([_pad_lanes(b_f[None, :]), _pad_lanes(dt_bias[None, :])], axis=1)
    w["a_log_row"] = _pad_lanes(a_log[None, :])
    w["d_skip_row"] = jnp.repeat(d_skip, SSD_HEADDIM)[None, :]
    for k, v in prm.items():
        w[k] = v.astype(BF16) if v.ndim == 2 and v.shape[0] >= D_MODEL else v
    return w


def _mixer_and_ffn(x, mod, w, tm, g_final, attend, conv_state, ssm_state, n_batch, seq, chunk):
    x, _ = _ffn(x, mod, 0, w["g_ffn1"], w["w1_ffn1"], w["w3_ffn1"], w["w2_ffn1"], g_final, tm, False)
    q, k32, v32, kb, vb, logf, dt = _inproj_a(x, mod, w["g_mix"], w["q"], w["k"], w["v"],
                                               w["small"], w["small_bias"], tm)
    z, xbc, ga, gs = _inproj_b(x, mod, w["g_mix"], w["z"], w["xbc"], w["ga"], w["gs"], tm)
    o = attend(q, kb, vb, logf)
    y, ssm_new, conv_new = _ssd(xbc, dt, conv_state, ssm_state, w["conv_w"], w["conv_b"],
                                w["a_log_row"], w["d_skip_row"], n_batch, seq, chunk)
    x = _merge(x, o, y, z, ga, gs, mod, w["g_ssd"], w["w_a"], w["w_s"], w["w_out"], tm)
    x, y_final = _ffn(x, mod, 6, w["g_ffn2"], w["w1_ffn2"], w["w3_ffn2"], w["w2_ffn2"], g_final, tm, True)
    return x, y_final, (k32, v32, logf, ssm_new, conv_new)


def kernel(x_prompt, x_sample, c_prompt, c_sample, cache_k, cache_v, cache_logf, state_ssm, state_conv,
           w_ada, b_ada, g_ffn1, w1_ffn1, w3_ffn1, w2_ffn1, g_mix, w_in, b_f, conv_w, conv_b,
           dt_bias, a_log, d_skip, g_ssd, w_a, w_s, w_out, g_ffn2, w1_ffn2, w3_ffn2, w2_ffn2, g_final):
    depth = w_ada.shape[0]
    bp, seq_p, _ = x_prompt.shape
    bs, seq_s, _ = x_sample.shape
    past = cache_k.shape[2]
    assert bp == 1
    n_pairs = W_ATT // LANES

    xp = x_prompt.reshape(bp * seq_p, D_MODEL)
    xs = x_sample.reshape(bs * seq_s, D_MODEL)
    c_rows = bs + bp
    c_all = jnp.pad(jnp.concatenate([c_sample, c_prompt], axis=0), ((0, -c_rows % 8), (0, 0)))
    gf = g_final[None, :]

    outs_p = [[] for _ in range(5)]
    outs_s = [[] for _ in range(5)]
    yp = ys = None
    for l in range(depth):
        prm = {"g_ffn1": g_ffn1[l][None, :], "w1_ffn1": w1_ffn1[l], "w3_ffn1": w3_ffn1[l], "w2_ffn1": w2_ffn1[l],
               "g_mix": g_mix[l][None, :], "conv_w": conv_w[l], "conv_b": conv_b[l][None, :],
               "g_ssd": g_ssd[l][None, :], "w_a": w_a[l], "w_s": w_s[l], "w_out": w_out[l],
               "g_ffn2": g_ffn2[l][None, :], "w1_ffn2": w1_ffn2[l], "w3_ffn2": w3_ffn2[l], "w2_ffn2": w2_ffn2[l]}
        w = _layer_weights(w_in[l], b_f[l], dt_bias[l], a_log[l], d_skip[l], prm)

        mod = _ada(c_all, w_ada[l], b_ada[l])
        mod_s = jnp.repeat(mod[:bs], seq_s, axis=0)
        mod_p = mod[bs:bs + 1]

        def attend_prompt(q, kb, vb, logf):
            f_cum = _fcum_prompt(logf.T)
            nb = seq_p // ATT_BLOCK
            f_blocks = f_cum.reshape(n_pairs, 2, nb, ATT_BLOCK).transpose(0, 2, 1, 3)
            f_blocks = jnp.pad(f_blocks, ((0, 0), (0, 0), (0, 6), (0, 0)))
            return _attn_prompt(q, kb, vb, f_blocks)

        def attend_sample(q, kb, vb, logf):
            past_rows = cache_logf[l].transpose(0, 2, 1).reshape(bs * N_HEADS_A, past)
            new_rows = logf.reshape(bs, seq_s, N_HEADS_A).transpose(0, 2, 1).reshape(bs * N_HEADS_A, seq_s)
            f_past, f_new = _fcum_sample(past_rows, _pad_lanes(new_rows))
            return _attn_sample(q, kb, vb,
                                cache_k[l].reshape(bs, past, W_ATT), cache_v[l].reshape(bs, past, W_ATT),
                                f_past.reshape(bs, n_pairs, 2, past), f_new.reshape(bs, n_pairs, 2, LANES),
                                bs, seq_s)

        zero_conv = jnp.zeros((bp, CONV_W - 1, CONV_DIM), F32)
        zero_ssm = jnp.zeros((bp, D_INNER, SSD_STATE), F32)
        xp, yp, new_p = _mixer_and_ffn(xp, mod_p, w, ROW_TILE, gf, attend_prompt, zero_conv, zero_ssm,
                                       bp, seq_p, SSD_CHUNK_PROMPT)
        xs, ys, new_s = _mixer_and_ffn(xs, mod_s, w, bs * seq_s, gf, attend_sample, state_conv[l],
                                       state_ssm[l].reshape(bs, D_INNER, SSD_STATE), bs, seq_s, seq_s)
        for dst, new, nb, sq in ((outs_p, new_p, bp, seq_p), (outs_s, new_s, bs, seq_s)):
            k32, v32, logf, ssm_new, conv_new = new
            dst[0].append(k32.reshape(nb, sq, N_HEADS_A, HEAD_DIM_A))
            dst[1].append(v32.reshape(nb, sq, N_HEADS_A, HEAD_DIM_A))
            dst[2].append(logf.reshape(nb, sq, N_HEADS_A))
            dst[3].append(ssm_new.reshape(nb, SSD_HEADS, SSD_HEADDIM, SSD_STATE))
            dst[4].append(conv_new)

    y_prompt = yp.reshape(bp, seq_p, D_MODEL)
    y_sample = ys.reshape(bs, seq_s, D_MODEL)
    return (y_prompt, y_sample, *[jnp.stack(o) for o in outs_p], *[jnp.stack(o) for o in outs_s])
```

```python
import functools

import jax
import jax.numpy as jnp
from jax import lax
from jax.experimental import pallas as pl
from jax.experimental.pallas import tpu as pltpu

F32 = jnp.float32
BF16 = jnp.bfloat16

EPS = 1e-6
D_MODEL = 1024
N_HEADS_A = 16
HEAD_DIM_A = 64
W_ATT = N_HEADS_A * HEAD_DIM_A
D_INNER = 2 * D_MODEL
SSD_HEADDIM = 64
SSD_HEADS = D_INNER // SSD_HEADDIM
SSD_STATE = 128
SSD_GROUPS = 4
CONV_W = 4
CONV_DIM = D_INNER + 2 * SSD_GROUPS * SSD_STATE
D_FF = 2816
N_MOD = 9

LANES = 128
VMEM_LIMIT_BYTES = 56 * 1024 * 1024

ROW_TILE = 512
FF_CHUNK = 1408
ATT_QUERY_BLOCK = 1024
ATT_KEY_BLOCK = 512
SSD_CHUNK_PROMPT = 128
CONV_PAD = 8

_NT = (((1,), (1,)), ((), ()))
_TN = (((0,), (0,)), ((), ()))


def _resident(shape):
    nd = len(shape)
    return pl.BlockSpec(shape, lambda *_: (0,) * nd, pipeline_mode=pl.Buffered(1))


def _params(n_axes):
    return pltpu.CompilerParams(dimension_semantics=("arbitrary",) * n_axes,
                                vmem_limit_bytes=VMEM_LIMIT_BYTES)


def _silu(x):
    return x * jax.nn.sigmoid(x)


def _softplus(x):
    return jnp.maximum(x, 0.0) + jnp.log1p(jnp.exp(-jnp.abs(x)))


def _split3(x):
    x1 = x.astype(BF16)
    r1 = x - x1.astype(F32)
    x2 = r1.astype(BF16)
    r2 = r1 - x2.astype(F32)
    return x1, x2, r2.astype(BF16)


def _dot01_right(x, u01):
    acc = None
    for piece in _split3(x):
        t = jnp.dot(piece, u01, preferred_element_type=F32)
        acc = t if acc is None else acc + t
    return acc


def _dot01_left(l01, x):
    acc = None
    for piece in _split3(x):
        t = jnp.dot(l01, piece, preferred_element_type=F32)
        acc = t if acc is None else acc + t
    return acc


def _transpose01(eye, x):
    acc = None
    for piece in _split3(x):
        t = lax.dot_general(eye, piece, _NT, preferred_element_type=F32)
        acc = t if acc is None else acc + t
    return acc


def _iota(shape, dim):
    return lax.broadcasted_iota(jnp.int32, shape, dim)


def _modulated_norm(x, g, shift, scale):
    ms = jnp.mean(x * x, axis=-1, keepdims=True)
    h = x * lax.rsqrt(ms + EPS) * g
    return h * (1.0 + scale) + shift


def _ada_kernel(c_ref, w_ref, b_ref, o_ref):
    c = _silu(c_ref[...]).astype(BF16)
    o_ref[...] = jnp.dot(c, w_ref[...].astype(BF16), preferred_element_type=F32) + b_ref[...]


def _ada(c_all, w_ada, b_ada):
    rows = c_all.shape[0]
    n = w_ada.shape[1]
    tn = 1536
    return pl.pallas_call(
        _ada_kernel,
        grid=(n // tn,),
        in_specs=[pl.BlockSpec((rows, D_MODEL), lambda j: (0, 0)),
                  pl.BlockSpec((D_MODEL, tn), lambda j: (0, j)),
                  pl.BlockSpec((1, tn), lambda j: (0, j))],
        out_specs=pl.BlockSpec((rows, tn), lambda j: (0, j)),
        out_shape=jax.ShapeDtypeStruct((rows, n), F32),
        compiler_params=_params(1),
        name="ada_mod",
    )(c_all, w_ada, b_ada.reshape(1, n))


def _mod_spec(mod, tm, k):
    if mod.shape[0] == 1:
        return pl.BlockSpec((1, D_MODEL), lambda i: (0, k))
    return pl.BlockSpec((tm, D_MODEL), lambda i: (i, k))


def _row_spec(tm, width):
    return pl.BlockSpec((tm, width), lambda i: (i, 0))


def _ffn_kernel(x_ref, sh_ref, sc_ref, ga_ref, g_ref, w1_ref, w3_ref, w2_ref, gf_ref, o_ref, *maybe_y_ref,
                final_norm):
    x = x_ref[...]
    h = _modulated_norm(x, g_ref[...], sh_ref[...], sc_ref[...]).astype(BF16)
    y = None
    for c0 in range(0, D_FF, FF_CHUNK):
        a = jnp.dot(h, w1_ref[:, c0:c0 + FF_CHUNK], preferred_element_type=F32)
        b = jnp.dot(h, w3_ref[:, c0:c0 + FF_CHUNK], preferred_element_type=F32)
        t = jnp.dot((_silu(a) * b).astype(BF16), w2_ref[c0:c0 + FF_CHUNK, :], preferred_element_type=F32)
        y = t if y is None else y + t
    out = x + 0.5 * (1.0 + ga_ref[...]) * y
    o_ref[...] = out
    if final_norm:
        (y_ref,) = maybe_y_ref
        ms = jnp.mean(out * out, axis=-1, keepdims=True)
        y_ref[...] = out * lax.rsqrt(ms + EPS) * gf_ref[...]


def _ffn(x, mod, k0, g, w1, w3, w2, g_final, tm, final_norm):
    rows = x.shape[0]
    n_out = 2 if final_norm else 1
    res = pl.pallas_call(
        functools.partial(_ffn_kernel, final_norm=final_norm),
        grid=(rows // tm,),
        in_specs=[_row_spec(tm, D_MODEL),
                  _mod_spec(mod, tm, k0), _mod_spec(mod, tm, k0 + 1), _mod_spec(mod, tm, k0 + 2),
                  _resident((1, D_MODEL)),
                  _resident((D_MODEL, D_FF)), _resident((D_MODEL, D_FF)), _resident((D_FF, D_MODEL)),
                  _resident((1, D_MODEL))],
        out_specs=[_row_spec(tm, D_MODEL)] * n_out,
        out_shape=[jax.ShapeDtypeStruct((rows, D_MODEL), F32)] * n_out,
        compiler_params=_params(1),
        name="ffn_final" if final_norm else "ffn",
    )(x, mod, mod, mod, g, w1, w3, w2, g_final)
    return (res[0], res[1]) if final_norm else (res[0], None)


N_FOLD = 3


def _fold_selectors():
    r = _iota((LANES, W_ATT), 0)
    j = _iota((LANES, W_ATT), 1)
    piece, head = r // N_HEADS_A, r % N_HEADS_A
    same_pair = (j // LANES) == (head // 2)
    valid = r < N_FOLD * N_HEADS_A
    lane = j % LANES
    even = valid & same_pair & (head % 2 == 0) & (lane == HEAD_DIM_A + piece)
    odd = valid & same_pair & (head % 2 == 1) & (lane == piece)
    return even.astype(BF16), odd.astype(BF16)


def _inproj_a_kernel(x_ref, sh_ref, sc_ref, g_ref, wq_ref, wk_ref, wv_ref, ws_ref, bs_ref, *refs, folded):
    u = _modulated_norm(x_ref[...], g_ref[...], sh_ref[...], sc_ref[...]).astype(BF16)
    q = jnp.dot(u, wq_ref[...], preferred_element_type=F32) * (HEAD_DIM_A ** -0.5)
    k = jnp.dot(u, wk_ref[...], preferred_element_type=F32)
    v = jnp.dot(u, wv_ref[...], preferred_element_type=F32)
    sm = jnp.dot(u, ws_ref[...], preferred_element_type=F32) + bs_ref[...]
    lane = _iota((1, LANES), 1)
    logf = jnp.where(lane < N_HEADS_A, -_softplus(-sm[:, :LANES]), 0.0)
    dt = jnp.where(lane < SSD_HEADS, _softplus(sm[:, LANES:]), 0.0)
    if not folded:
        q_ref, kb_ref, vb_ref, k32_ref, v32_ref, lf_ref, dt_ref = refs
        q_ref[...] = q.astype(BF16)
        kb_ref[...] = k.astype(BF16)
        vb_ref[...] = v.astype(BF16)
    else:
        qe_ref, qo_ref, ke_ref, ko_ref, ve_ref, vo_ref, k32_ref, v32_ref, lf_ref, dt_ref, carry_sc = refs
        tm = x_ref.shape[0]

        @pl.when(pl.program_id(0) == 0)
        def _():
            carry_sc[...] = jnp.zeros(carry_sc.shape, F32)

        lower = (_iota((tm, tm), 1) <= _iota((tm, tm), 0)).astype(BF16)
        f_cum = _dot01_left(lower, logf) + carry_sc[0:1, :]
        carry_sc[...] = jnp.broadcast_to(f_cum[tm - 1:tm, :], carry_sc.shape)
        hi, mid, lo = _split3(-f_cum)
        packed = (hi.astype(F32) + pltpu.roll(mid.astype(F32), N_HEADS_A, 1)
                  + pltpu.roll(lo.astype(F32), 2 * N_HEADS_A, 1)).astype(BF16)
        sel_even, sel_odd = _fold_selectors()
        fold_even = jnp.dot(packed, sel_even, preferred_element_type=F32)
        fold_odd = jnp.dot(packed, sel_odd, preferred_element_type=F32)
        lip = _iota((1, W_ATT), 1) % LANES
        in_even, in_odd = lip < HEAD_DIM_A, lip >= HEAD_DIM_A
        ones_even = ((lip >= HEAD_DIM_A) & (lip < HEAD_DIM_A + N_FOLD)).astype(F32)
        ones_odd = (lip < N_FOLD).astype(F32)
        qe_ref[...] = jnp.where(in_even, q, ones_even).astype(BF16)
        qo_ref[...] = jnp.where(in_odd, q, ones_odd).astype(BF16)
        ke_ref[...] = jnp.where(in_even, k, fold_even).astype(BF16)
        ko_ref[...] = jnp.where(in_odd, k, fold_odd).astype(BF16)
        ve_ref[...] = jnp.where(in_even, v, (lip == HEAD_DIM_A).astype(F32)).astype(BF16)
        vo_ref[...] = jnp.where(in_odd, v, (lip == 0).astype(F32)).astype(BF16)
    k32_ref[...] = k
    v32_ref[...] = v
    lf_ref[...] = logf[:, :N_HEADS_A]
    dt_ref[...] = dt


def _inproj_a(x, mod, g, wq, wk, wv, ws, bs, tm, folded):
    rows = x.shape[0]
    sd = jax.ShapeDtypeStruct
    n_bf = 6 if folded else 3
    return pl.pallas_call(
        functools.partial(_inproj_a_kernel, folded=folded),
        grid=(rows // tm,),
        in_specs=[_row_spec(tm, D_MODEL), _mod_spec(mod, tm, 3), _mod_spec(mod, tm, 4),
                  _resident((1, D_MODEL)),
                  _resident((D_MODEL, W_ATT)), _resident((D_MODEL, W_ATT)), _resident((D_MODEL, W_ATT)),
                  _resident((D_MODEL, 2 * LANES)), _resident((1, 2 * LANES))],
        out_specs=[_row_spec(tm, W_ATT)] * (n_bf + 2) + [_row_spec(tm, N_HEADS_A), _row_spec(tm, LANES)],
        out_shape=[sd((rows, W_ATT), BF16)] * n_bf + [sd((rows, W_ATT), F32)] * 2
                  + [sd((rows, N_HEADS_A), F32), sd((rows, LANES), F32)],
        scratch_shapes=[pltpu.VMEM((8, LANES), F32)] if folded else [],
        compiler_params=_params(1),
        name="inproj_qkv_folded" if folded else "inproj_qkv",
    )(x, mod, mod, g, wq, wk, wv, ws, bs)


def _inproj_b_kernel(x_ref, sh_ref, sc_ref, g_ref, wz_ref, wx_ref, wga_ref, wgs_ref,
                     z_ref, xbc_ref, ga_ref, gs_ref):
    u = _modulated_norm(x_ref[...], g_ref[...], sh_ref[...], sc_ref[...]).astype(BF16)
    z_ref[...] = jnp.dot(u, wz_ref[...], preferred_element_type=F32).astype(BF16)
    xbc_ref[...] = jnp.dot(u, wx_ref[...], preferred_element_type=F32)
    ga_ref[...] = jnp.dot(u, wga_ref[...], preferred_element_type=F32).astype(BF16)
    gs_ref[...] = jnp.dot(u, wgs_ref[...], preferred_element_type=F32).astype(BF16)


def _inproj_b(x, mod, g, wz, wx, wga, wgs, tm):
    rows = x.shape[0]
    sd = jax.ShapeDtypeStruct
    return pl.pallas_call(
        _inproj_b_kernel,
        grid=(rows // tm,),
        in_specs=[_row_spec(tm, D_MODEL), _mod_spec(mod, tm, 3), _mod_spec(mod, tm, 4),
                  _resident((1, D_MODEL)),
                  _resident((D_MODEL, D_INNER)), _resident((D_MODEL, CONV_DIM)),
                  _resident((D_MODEL, D_MODEL)), _resident((D_MODEL, D_MODEL))],
        out_specs=[_row_spec(tm, D_INNER), _row_spec(tm, CONV_DIM),
                   _row_spec(tm, D_MODEL), _row_spec(tm, D_MODEL)],
        out_shape=[sd((rows, D_INNER), BF16), sd((rows, CONV_DIM), F32),
                   sd((rows, D_MODEL), BF16), sd((rows, D_MODEL), BF16)],
        compiler_params=_params(1),
        name="inproj_ssd",
    )(x, mod, mod, g, wz, wx, wga, wgs)


def _merge_kernel(x_ref, o_ref, y_ref, z_ref, ga_ref, gs_ref, g2_ref, gssd_ref, wa_ref, ws_ref, wo_ref,
                  out_ref):
    yg = y_ref[...].astype(F32) * _silu(z_ref[...].astype(F32))
    ms = jnp.mean(yg * yg, axis=-1, keepdims=True)
    yn = (yg * lax.rsqrt(ms + EPS) * gssd_ref[...]).astype(BF16)
    o_s = jnp.dot(yn, ws_ref[...], preferred_element_type=F32)
    o_a = jnp.dot(o_ref[...], wa_ref[...], preferred_element_type=F32)
    merged = (jax.nn.sigmoid(ga_ref[...].astype(F32)) * o_a
              + jax.nn.sigmoid(gs_ref[...].astype(F32)) * o_s)
    m = jnp.dot(merged.astype(BF16), wo_ref[...], preferred_element_type=F32)
    out_ref[...] = x_ref[...] + (1.0 + g2_ref[...]) * m


def _merge(x, o, y, z, ga, gs, mod, g_ssd, wa, ws, wo, tm):
    rows = x.shape[0]
    return pl.pallas_call(
        _merge_kernel,
        grid=(rows // tm,),
        in_specs=[_row_spec(tm, D_MODEL), _row_spec(tm, W_ATT), _row_spec(tm, D_INNER),
                  _row_spec(tm, D_INNER), _row_spec(tm, D_MODEL), _row_spec(tm, D_MODEL),
                  _mod_spec(mod, tm, 5), _resident((1, D_INNER)),
                  _resident((W_ATT, D_MODEL)), _resident((D_INNER, D_MODEL)), _resident((D_MODEL, D_MODEL))],
        out_specs=_row_spec(tm, D_MODEL),
        out_shape=jax.ShapeDtypeStruct((rows, D_MODEL), F32),
        compiler_params=_params(1),
        name="merge",
    )(x, o, y, z, ga, gs, mod, g_ssd, wa, ws, wo)


def _fcum_sample_kernel(past_ref, new_ref, fp_ref, fn_ref):
    p = past_ref.shape[1]
    upper_p = (_iota((p, p), 0) <= _iota((p, p), 1)).astype(BF16)
    upper_n = (_iota((LANES, LANES), 0) <= _iota((LANES, LANES), 1)).astype(BF16)
    past = past_ref[...]
    fp_ref[...] = _dot01_right(past, upper_p)
    total = _dot01_right(past, jnp.ones((p, LANES), BF16))
    fn_ref[...] = total + _dot01_right(new_ref[...], upper_n)


def _fcum_sample(past, new):
    rows, p = past.shape
    return pl.pallas_call(
        _fcum_sample_kernel,
        out_shape=[jax.ShapeDtypeStruct((rows, p), F32), jax.ShapeDtypeStruct((rows, LANES), F32)],
        compiler_params=pltpu.CompilerParams(vmem_limit_bytes=VMEM_LIMIT_BYTES),
        name="fcum_sample",
    )(past, new)


def _pair_split(x):
    lane = _iota((1, LANES), 1)
    keep0 = (lane < HEAD_DIM_A).astype(F32).astype(x.dtype)
    keep1 = (lane >= HEAD_DIM_A).astype(F32).astype(x.dtype)
    return x * keep0, x * keep1


def _pair_value_operands(v):
    lane = _iota((1, LANES), 1)
    v0, v1 = _pair_split(v)
    ones0 = (lane == HEAD_DIM_A).astype(F32).astype(v.dtype)
    ones1 = (lane == 0).astype(F32).astype(v.dtype)
    return v0 + ones0, v1 + ones1


def _pair_normalize(acc_even, acc_odd):
    lane = _iota((1, LANES), 1)
    return jnp.where(lane < HEAD_DIM_A,
                     acc_even / acc_even[:, HEAD_DIM_A:HEAD_DIM_A + 1], acc_odd / acc_odd[:, 0:1])


def _attn_prompt_kernel(qe_ref, qo_ref, ke_ref, ko_ref, ve_ref, vo_ref, o_ref, m_sc, acc_sc):
    qi = pl.program_id(1)
    tq = qe_ref.shape[0]
    tk = ATT_KEY_BLOCK
    blocks_per_q = tq // tk
    q_refs, k_refs, v_refs = (qe_ref, qo_ref), (ke_ref, ko_ref), (ve_ref, vo_ref)
    m_sc[...] = jnp.full(m_sc.shape, -jnp.inf, F32)
    acc_sc[...] = jnp.zeros(acc_sc.shape, F32)

    def unit(r0, nrows, j, key_start):
        off = pl.multiple_of(j * tk, tk)
        rows = slice(r0, r0 + nrows)
        scores = []
        for hh in range(2):
            s = lax.dot_general(q_refs[hh][rows, :], k_refs[hh][pl.ds(off, tk), :], _NT,
                                preferred_element_type=F32)
            if key_start is not None:
                keep = (key_start + _iota((nrows, tk), 1)) <= (r0 + _iota((nrows, tk), 0))
                s = jnp.where(keep, s, -jnp.inf)
            scores.append(s)
        probs = []
        for hh in range(2):
            m_prev = m_sc[hh, rows, :]
            m_new = jnp.maximum(m_prev, jnp.max(scores[hh], axis=1, keepdims=True))
            p = jnp.exp(scores[hh] - jnp.concatenate([m_new] * (tk // LANES), axis=1))
            probs.append((p.astype(BF16), jnp.exp(m_prev - m_new)))
            m_sc[hh, rows, :] = m_new
        for hh in range(2):
            p, alpha = probs[hh]
            pv = jnp.dot(p, v_refs[hh][pl.ds(off, tk), :], preferred_element_type=F32)
            acc_sc[hh, rows, :] = alpha * acc_sc[hh, rows, :] + pv

    def body(j, carry):
        unit(0, tq, j, None)
        return carry

    first_diag = qi * blocks_per_q
    lax.fori_loop(0, first_diag, body, 0)
    for d in range(blocks_per_q):
        unit(d * tk, tq - d * tk, first_diag + d, d * tk)

    o_ref[...] = _pair_normalize(acc_sc[0], acc_sc[1]).astype(o_ref.dtype)


def _attn_prompt(qe, qo, ke, ko, ve, vo):
    s = qe.shape[0]
    tq = ATT_QUERY_BLOCK
    n_pairs = W_ATT // LANES
    q_spec = pl.BlockSpec((tq, LANES), lambda p, i: (i, p))
    kv_spec = pl.BlockSpec((s, LANES), lambda p, i: (0, p))
    return pl.pallas_call(
        _attn_prompt_kernel,
        grid=(n_pairs, s // tq),
        in_specs=[q_spec, q_spec, kv_spec, kv_spec, kv_spec, kv_spec],
        out_specs=q_spec,
        out_shape=jax.ShapeDtypeStruct((s, W_ATT), BF16),
        scratch_shapes=[pltpu.VMEM((2, tq, LANES), F32), pltpu.VMEM((2, tq, LANES), F32)],
        compiler_params=_params(2),
        name="attn_prompt",
    )(qe, qo, ke, ko, ve, vo)


def _attn_sample_kernel(q_ref, kn_ref, vn_ref, kc_ref, vc_ref, fp_ref, fn_ref, o_ref):
    n = q_ref.shape[0]
    qm = _pair_split(q_ref[...])
    kc = kc_ref[...].astype(BF16)
    kn = kn_ref[...]
    vc_ops = _pair_value_operands(vc_ref[...].astype(BF16))
    vn_ops = _pair_value_operands(vn_ref[...])
    fp = fp_ref[...]
    fn = fn_ref[...][:, :n]
    keep = _iota((n, n), 1) <= _iota((n, n), 0)
    accs = []
    for hh in range(2):
        s_past = lax.dot_general(qm[hh], kc, _NT, preferred_element_type=F32) - fp[hh:hh + 1, :]
        s_new = lax.dot_general(qm[hh], kn, _NT, preferred_element_type=F32) - fn[hh:hh + 1, :]
        s_new = jnp.where(keep, s_new, -jnp.inf)
        m = jnp.maximum(jnp.max(s_past, axis=1, keepdims=True), jnp.max(s_new, axis=1, keepdims=True))
        p_past = jnp.exp(s_past - m).astype(BF16)
        p_new = jnp.exp(s_new - m).astype(BF16)
        accs.append(jnp.dot(p_past, vc_ops[hh], preferred_element_type=F32)
                    + jnp.dot(p_new, vn_ops[hh], preferred_element_type=F32))
    o_ref[...] = _pair_normalize(*accs).astype(o_ref.dtype)


def _attn_sample(q, kn, vn, cache_k, cache_v, f_past, f_new, n_batch, n_new):
    past = cache_k.shape[1]
    n_pairs = W_ATT // LANES
    tok = pl.BlockSpec((n_new, LANES), lambda b, p: (b, p))
    cache = pl.BlockSpec((None, past, LANES), lambda b, p: (b, 0, p))
    return pl.pallas_call(
        _attn_sample_kernel,
        grid=(n_batch, n_pairs),
        in_specs=[tok, tok, tok, cache, cache,
                  pl.BlockSpec((None, None, 2, past), lambda b, p: (b, p, 0, 0)),
                  pl.BlockSpec((None, None, 2, LANES), lambda b, p: (b, p, 0, 0))],
        out_specs=tok,
        out_shape=jax.ShapeDtypeStruct((n_batch * n_new, W_ATT), BF16),
        compiler_params=_params(2),
        name="attn_sample",
    )(q, kn, vn, cache_k, cache_v, f_past, f_new)


def _ssd_kernel(xbc_ref, dt_ref, cst_ref, h0_ref, cw_ref, cb_ref, alog_ref, dsk_ref,
                y_ref, hout_ref, cout_ref, ht_sc, cbuf_sc):
    c = pl.program_id(1)
    n_chunks = pl.num_programs(1)
    t = xbc_ref.shape[0]
    hp_group = D_INNER // SSD_GROUPS
    first_tail_row = CONV_PAD - (CONV_W - 1)

    @pl.when(c == 0)
    def _():
        cbuf_sc[first_tail_row:CONV_PAD, :] = cst_ref[...]
        for b in range(D_INNER // LANES):
            ht_sc[:, b * LANES:(b + 1) * LANES] = h0_ref[b * LANES:(b + 1) * LANES, :].T

    cbuf_sc[CONV_PAD:CONV_PAD + t, :] = xbc_ref[...]
    conv = cb_ref[...] + cw_ref[0:1, :] * cbuf_sc[first_tail_row:first_tail_row + t, :]
    for i in range(1, CONV_W):
        conv = conv + cw_ref[i:i + 1, :] * cbuf_sc[first_tail_row + i:first_tail_row + i + t, :]
    xc = _silu(conv)
    tail = cbuf_sc[t + first_tail_row:t + CONV_PAD, :]
    cbuf_sc[first_tail_row:CONV_PAD, :] = tail

    @pl.when(c == n_chunks - 1)
    def _():
        cout_ref[...] = tail

    xs = xc[:, :D_INNER]
    b_all = xc[:, D_INNER:D_INNER + SSD_GROUPS * SSD_STATE].astype(BF16)
    c_all = xc[:, D_INNER + SSD_GROUPS * SSD_STATE:].astype(BF16)

    dt = dt_ref[...]
    a_row = -jnp.exp(alog_ref[...])
    lower = (_iota((t, t), 1) <= _iota((t, t), 0)).astype(BF16)
    acs = _dot01_left(lower, dt * a_row)
    acs_end = acs[t - 1:t, :]
    compact = jnp.concatenate([jnp.exp(acs_end - acs) * dt, dt, jnp.exp(acs)], axis=0)
    expand = ((_iota((LANES, D_INNER), 1) // SSD_HEADDIM) == _iota((LANES, D_INNER), 0)).astype(BF16)
    wide = _dot01_right(compact, expand)
    xs_state = (xs * wide[0:t]).astype(BF16)
    xs_dt = (xs * wide[t:2 * t]).astype(BF16)
    e_acs = wide[2 * t:3 * t]
    chunk_decay = e_acs[t - 1:t, :]

    eye = (_iota((LANES, LANES), 0) == _iota((LANES, LANES), 1)).astype(BF16)
    acs_t = _transpose01(eye, acs)
    causal = _iota((t, t), 1) <= _iota((t, t), 0)

    for g in range(SSD_GROUPS):
        b_g = b_all[:, g * SSD_STATE:(g + 1) * SSD_STATE]
        c_g = c_all[:, g * SSD_STATE:(g + 1) * SSD_STATE]
        cols = slice(g * hp_group, (g + 1) * hp_group)
        ht_g = ht_sc[:, cols]
        y_off = jnp.dot(c_g, ht_g.astype(BF16), preferred_element_type=F32) * e_acs[:, cols]
        states = lax.dot_general(b_g, xs_state[:, cols], _TN, preferred_element_type=F32)
        ht_sc[:, cols] = ht_g * chunk_decay[:, cols] + states
        cb = lax.dot_general(c_g, b_g, _NT, preferred_element_type=F32)
        heads_per_group = SSD_HEADS // SSD_GROUPS
        for pair in range(heads_per_group // 2):
            h0 = g * heads_per_group + 2 * pair
            lo = h0 * SSD_HEADDIM
            x_ops = _pair_split(xs_dt[:, lo:lo + LANES])
            y_pair = y_off[:, lo - g * hp_group:lo - g * hp_group + LANES]
            for hh in range(2):
                h = h0 + hh
                seg = acs[:, h:h + 1] - acs_t[h:h + 1, :]
                w = (cb * jnp.exp(jnp.where(causal, seg, -jnp.inf))).astype(BF16)
                y_pair = y_pair + jnp.dot(w, x_ops[hh], preferred_element_type=F32)
            y_pair = y_pair + dsk_ref[:, lo:lo + LANES] * xs[:, lo:lo + LANES]
            y_ref[:, lo:lo + LANES] = y_pair.astype(y_ref.dtype)

    @pl.when(c == n_chunks - 1)
    def _():
        for b in range(D_INNER // LANES):
            hout_ref[b * LANES:(b + 1) * LANES, :] = ht_sc[:, b * LANES:(b + 1) * LANES].T


def _ssd(xbc, dt, conv_state, ssm_state, conv_w, conv_b, a_log_row, d_skip_row, n_batch, seq, chunk):
    assert seq % chunk == 0 and chunk >= CONV_W - 1
    nc = seq // chunk
    rows = lambda b, c: (b * nc + c, 0)
    per_batch3 = lambda b, c: (b, 0, 0)
    sd = jax.ShapeDtypeStruct
    return pl.pallas_call(
        _ssd_kernel,
        grid=(n_batch, nc),
        in_specs=[pl.BlockSpec((chunk, CONV_DIM), rows),
                  pl.BlockSpec((chunk, LANES), rows),
                  pl.BlockSpec((None, CONV_W - 1, CONV_DIM), per_batch3),
                  pl.BlockSpec((None, D_INNER, SSD_STATE), per_batch3),
                  pl.BlockSpec((CONV_W, CONV_DIM), lambda b, c: (0, 0)),
                  pl.BlockSpec((1, CONV_DIM), lambda b, c: (0, 0)),
                  pl.BlockSpec((1, LANES), lambda b, c: (0, 0)),
                  pl.BlockSpec((1, D_INNER), lambda b, c: (0, 0))],
        out_specs=[pl.BlockSpec((chunk, D_INNER), rows),
                   pl.BlockSpec((None, D_INNER, SSD_STATE), per_batch3),
                   pl.BlockSpec((None, CONV_W - 1, CONV_DIM), per_batch3)],
        out_shape=[sd((n_batch * seq, D_INNER), BF16),
                   sd((n_batch, D_INNER, SSD_STATE), F32),
                   sd((n_batch, CONV_W - 1, CONV_DIM), F32)],
        scratch_shapes=[pltpu.VMEM((SSD_STATE, D_INNER), F32),
                        pltpu.VMEM((CONV_PAD + chunk, CONV_DIM), F32)],
        compiler_params=_params(2),
        name="ssd_scan",
    )(xbc, dt, conv_state, ssm_state, conv_w, conv_b, a_log_row, d_skip_row)


def _pad_lanes(row, width=LANES):
    return jnp.pad(row, ((0, 0), (0, width - row.shape[1])))


def _layer_weights(w_in, b_f, dt_bias, a_log, d_skip, prm):
    o = 0
    pieces = {}
    for name, width in (("q", W_ATT), ("k", W_ATT), ("v", W_ATT), ("f", N_HEADS_A), ("z", D_INNER),
                        ("xbc", CONV_DIM), ("dt", SSD_HEADS), ("ga", D_MODEL), ("gs", D_MODEL)):
        pieces[name] = w_in[:, o:o + width]
        o += width
    w = {k: pieces[k].astype(BF16) for k in ("q", "k", "v", "z", "xbc", "ga", "gs")}
    w["small"] = jnp.concatenate([_pad_lanes(pieces["f"]), _pad_lanes(pieces["dt"])], axis=1).astype(BF16)
    w["small_bias"] = jnp.concatenate([_pad_lanes(b_f[None, :]), _pad_lanes(dt_bias[None, :])], axis=1)
    w["a_log_row"] = _pad_lanes(a_log[None, :])
    w["d_skip_row"] = jnp.repeat(d_skip, SSD_HEADDIM)[None, :]
    for k, v in prm.items():
        w[k] = v.astype(BF16) if v.ndim == 2 and v.shape[0] >= D_MODEL else v
    return w


def _mixer_and_ffn(x, mod, w, tm, g_final, attend, folded, conv_state, ssm_state, n_batch, seq, chunk):
    x, _ = _ffn(x, mod, 0, w["g_ffn1"], w["w1_ffn1"], w["w3_ffn1"], w["w2_ffn1"], g_final, tm, False)
    *att_ops, k32, v32, logf, dt = _inproj_a(x, mod, w["g_mix"], w["q"], w["k"], w["v"],
                                             w["small"], w["small_bias"], tm, folded)
    z, xbc, ga, gs = _inproj_b(x, mod, w["g_mix"], w["z"], w["xbc"], w["ga"], w["gs"], tm)
    o = attend(att_ops, logf)
    y, ssm_new, conv_new = _ssd(xbc, dt, conv_state, ssm_state, w["conv_w"], w["conv_b"],
                                w["a_log_row"], w["d_skip_row"], n_batch, seq, chunk)
    x = _merge(x, o, y, z, ga, gs, mod, w["g_ssd"], w["w_a"], w["w_s"], w["w_out"], tm)
    x, y_final = _ffn(x, mod, 6, w["g_ffn2"], w["w1_ffn2"], w["w3_ffn2"], w["w2_ffn2"], g_final, tm, True)
    return x, y_final, (k32, v32, logf, ssm_new, conv_new)


def kernel(x_prompt, x_sample, c_prompt, c_sample, cache_k, cache_v, cache_logf, state_ssm, state_conv,
           w_ada, b_ada, g_ffn1, w1_ffn1, w3_ffn1, w2_ffn1, g_mix, w_in, b_f, conv_w, conv_b,
           dt_bias, a_log, d_skip, g_ssd, w_a, w_s, w_out, g_ffn2, w1_ffn2, w3_ffn2, w2_ffn2, g_final):
    depth = w_ada.shape[0]
    bp, seq_p, _ = x_prompt.shape
    bs, seq_s, _ = x_sample.shape
    past = cache_k.shape[2]
    assert bp == 1
    n_pairs = W_ATT // LANES

    xp = x_prompt.reshape(bp * seq_p, D_MODEL)
    xs = x_sample.reshape(bs * seq_s, D_MODEL)
    c_rows = bs + bp
    c_all = jnp.pad(jnp.concatenate([c_sample, c_prompt], axis=0), ((0, -c_rows % 8), (0, 0)))
    gf = g_final[None, :]

    outs_p = [[] for _ in range(5)]
    outs_s = [[] for _ in range(5)]
    yp = ys = None
    for l in range(depth):
        prm = {"g_ffn1": g_ffn1[l][None, :], "w1_ffn1": w1_ffn1[l], "w3_ffn1": w3_ffn1[l], "w2_ffn1": w2_ffn1[l],
               "g_mix": g_mix[l][None, :], "conv_w": conv_w[l], "conv_b": conv_b[l][None, :],
               "g_ssd": g_ssd[l][None, :], "w_a": w_a[l], "w_s": w_s[l], "w_out": w_out[l],
               "g_ffn2": g_ffn2[l][None, :], "w1_ffn2": w1_ffn2[l], "w3_ffn2": w3_ffn2[l], "w2_ffn2": w2_ffn2[l]}
        w = _layer_weights(w_in[l], b_f[l], dt_bias[l], a_log[l], d_skip[l], prm)

        mod = _ada(c_all, w_ada[l], b_ada[l])
        mod_s = jnp.repeat(mod[:bs], seq_s, axis=0)
        mod_p = mod[bs:bs + 1]

        def attend_prompt(att_ops, logf):
            return _attn_prompt(*att_ops)

        def attend_sample(att_ops, logf):
            q, kb, vb = att_ops
            past_rows = cache_logf[l].transpose(0, 2, 1).reshape(bs * N_HEADS_A, past)
            new_rows = logf.reshape(bs, seq_s, N_HEADS_A).transpose(0, 2, 1).reshape(bs * N_HEADS_A, seq_s)
            f_past, f_new = _fcum_sample(past_rows, _pad_lanes(new_rows))
            return _attn_sample(q, kb, vb,
                                cache_k[l].reshape(bs, past, W_ATT), cache_v[l].reshape(bs, past, W_ATT),
                                f_past.reshape(bs, n_pairs, 2, past), f_new.reshape(bs, n_pairs, 2, LANES),
                                bs, seq_s)

        zero_conv = jnp.zeros((bp, CONV_W - 1, CONV_DIM), F32)
        zero_ssm = jnp.zeros((bp, D_INNER, SSD_STATE), F32)
        xp, yp, new_p = _mixer_and_ffn(xp, mod_p, w, ROW_TILE, gf, attend_prompt, True, zero_conv, zero_ssm,
                                       bp, seq_p, SSD_CHUNK_PROMPT)
        xs, ys, new_s = _mixer_and_ffn(xs, mod_s, w, bs * seq_s, gf, attend_sample, False, state_conv[l],
                                       state_ssm[l].reshape(bs, D_INNER, SSD_STATE), bs, seq_s, seq_s)
        for dst, new, nb, sq in ((outs_p, new_p, bp, seq_p), (outs_s, new_s, bs, seq_s)):
            k32, v32, logf, ssm_new, conv_new = new
            dst[0].append(k32.reshape(nb, sq, N_HEADS_A, HEAD_DIM_A))
            dst[1].append(v32.reshape(nb, sq, N_HEADS_A, HEAD_DIM_A))
            dst[2].append(logf.reshape(nb, sq, N_HEADS_A))
            dst[3].append(ssm_new.reshape(nb, SSD_HEADS, SSD_HEADDIM, SSD_STATE))
            dst[4].append(conv_new)

    y_prompt = yp.reshape(bp, seq_p, D_MODEL)
    y_sample = ys.reshape(bs, seq_s, D_MODEL)
    return (y_prompt, y_sample, *[jnp.stack(o) for o in outs_p], *[jnp.stack(o) for o in outs_s])
```

```python
import functools

import jax
import jax.numpy as jnp
from jax import lax
from jax.experimental import pallas as pl
from jax.experimental.pallas import tpu as pltpu

F32 = jnp.float32
BF16 = jnp.bfloat16

EPS = 1e-6
D_MODEL = 1024
N_HEADS_A = 16
HEAD_DIM_A = 64
W_ATT = N_HEADS_A * HEAD_DIM_A
D_INNER = 2 * D_MODEL
SSD_HEADDIM = 64
SSD_HEADS = D_INNER // SSD_HEADDIM
SSD_STATE = 128
SSD_GROUPS = 4
CONV_W = 4
CONV_DIM = D_INNER + 2 * SSD_GROUPS * SSD_STATE
D_FF = 2816
N_MOD = 9

LANES = 128
VMEM_LIMIT_BYTES = 56 * 1024 * 1024

ROW_TILE = 512
FF_CHUNK = 1408
ATT_QUERY_BLOCK = 1024
ATT_KEY_BLOCK = 512
SSD_CHUNK_PROMPT = 128
CONV_PAD = 8

_NT = (((1,), (1,)), ((), ()))
_TN = (((0,), (0,)), ((), ()))


def _resident(shape):
    nd = len(shape)
    return pl.BlockSpec(shape, lambda *_: (0,) * nd, pipeline_mode=pl.Buffered(1))


def _params(n_axes):
    return pltpu.CompilerParams(dimension_semantics=("arbitrary",) * n_axes,
                                vmem_limit_bytes=VMEM_LIMIT_BYTES)


def _silu(x):
    return x * jax.nn.sigmoid(x)


def _softplus(x):
    return jnp.maximum(x, 0.0) + jnp.log1p(jnp.exp(-jnp.abs(x)))


def _split3(x, pieces=3):
    out = []
    for _ in range(pieces - 1):
        hi = x.astype(BF16)
        out.append(hi)
        x = x - hi.astype(F32)
    out.append(x.astype(BF16))
    return out


def _dot01_right(x, u01, pieces=3):
    acc = None
    for piece in _split3(x, pieces):
        t = jnp.dot(piece, u01, preferred_element_type=F32)
        acc = t if acc is None else acc + t
    return acc


def _dot01_left(l01, x):
    acc = None
    for piece in _split3(x):
        t = jnp.dot(l01, piece, preferred_element_type=F32)
        acc = t if acc is None else acc + t
    return acc


def _transpose01(eye, x):
    acc = None
    for piece in _split3(x):
        t = lax.dot_general(eye, piece, _NT, preferred_element_type=F32)
        acc = t if acc is None else acc + t
    return acc


def _iota(shape, dim):
    return lax.broadcasted_iota(jnp.int32, shape, dim)


def _modulated_norm(x, g, shift, scale):
    ms = jnp.mean(x * x, axis=-1, keepdims=True)
    h = x * lax.rsqrt(ms + EPS) * g
    return h * (1.0 + scale) + shift


def _ada_kernel(c_ref, w_ref, b_ref, o_ref):
    c = _silu(c_ref[...]).astype(BF16)
    o_ref[...] = jnp.dot(c, w_ref[...].astype(BF16), preferred_element_type=F32) + b_ref[...]


def _ada(c_all, w_ada, b_ada):
    rows = c_all.shape[0]
    n = w_ada.shape[1]
    tn = 1536
    return pl.pallas_call(
        _ada_kernel,
        grid=(n // tn,),
        in_specs=[pl.BlockSpec((rows, D_MODEL), lambda j: (0, 0)),
                  pl.BlockSpec((D_MODEL, tn), lambda j: (0, j)),
                  pl.BlockSpec((1, tn), lambda j: (0, j))],
        out_specs=pl.BlockSpec((rows, tn), lambda j: (0, j)),
        out_shape=jax.ShapeDtypeStruct((rows, n), F32),
        compiler_params=_params(1),
        name="ada_mod",
    )(c_all, w_ada, b_ada.reshape(1, n))


def _mod_spec(mod, tm, k):
    if mod.shape[0] == 1:
        return pl.BlockSpec((1, D_MODEL), lambda i: (0, k))
    return pl.BlockSpec((tm, D_MODEL), lambda i: (i, k))


def _row_spec(tm, width):
    return pl.BlockSpec((tm, width), lambda i: (i, 0))


def _ffn_kernel(x_ref, sh_ref, sc_ref, ga_ref, g_ref, w1_ref, w3_ref, w2_ref, gf_ref, o_ref, *maybe_y_ref,
                final_norm):
    x = x_ref[...]
    h = _modulated_norm(x, g_ref[...], sh_ref[...], sc_ref[...]).astype(BF16)
    y = None
    for c0 in range(0, D_FF, FF_CHUNK):
        a = jnp.dot(h, w1_ref[:, c0:c0 + FF_CHUNK], preferred_element_type=F32)
        b = jnp.dot(h, w3_ref[:, c0:c0 + FF_CHUNK], preferred_element_type=F32)
        t = jnp.dot((_silu(a) * b).astype(BF16), w2_ref[c0:c0 + FF_CHUNK, :], preferred_element_type=F32)
        y = t if y is None else y + t
    out = x + 0.5 * (1.0 + ga_ref[...]) * y
    o_ref[...] = out
    if final_norm:
        (y_ref,) = maybe_y_ref
        ms = jnp.mean(out * out, axis=-1, keepdims=True)
        y_ref[...] = out * lax.rsqrt(ms + EPS) * gf_ref[...]


def _ffn(x, mod, k0, g, w1, w3, w2, g_final, tm, final_norm):
    rows = x.shape[0]
    n_out = 2 if final_norm else 1
    res = pl.pallas_call(
        functools.partial(_ffn_kernel, final_norm=final_norm),
        grid=(rows // tm,),
        in_specs=[_row_spec(tm, D_MODEL),
                  _mod_spec(mod, tm, k0), _mod_spec(mod, tm, k0 + 1), _mod_spec(mod, tm, k0 + 2),
                  _resident((1, D_MODEL)),
                  _resident((D_MODEL, D_FF)), _resident((D_MODEL, D_FF)), _resident((D_FF, D_MODEL)),
                  _resident((1, D_MODEL))],
        out_specs=[_row_spec(tm, D_MODEL)] * n_out,
        out_shape=[jax.ShapeDtypeStruct((rows, D_MODEL), F32)] * n_out,
        compiler_params=_params(1),
        name="ffn_final" if final_norm else "ffn",
    )(x, mod, mod, mod, g, w1, w3, w2, g_final)
    return (res[0], res[1]) if final_norm else (res[0], None)


N_FOLD = 3


def _fold_selectors():
    r = _iota((LANES, W_ATT), 0)
    j = _iota((LANES, W_ATT), 1)
    piece, head = r // N_HEADS_A, r % N_HEADS_A
    same_pair = (j // LANES) == (head // 2)
    valid = r < N_FOLD * N_HEADS_A
    lane = j % LANES
    even = valid & same_pair & (head % 2 == 0) & (lane == HEAD_DIM_A + piece)
    odd = valid & same_pair & (head % 2 == 1) & (lane == piece)
    return even.astype(BF16), odd.astype(BF16)


def _inproj_a_kernel(x_ref, sh_ref, sc_ref, g_ref, wq_ref, wk_ref, wv_ref, ws_ref, bs_ref, *refs, folded):
    u = _modulated_norm(x_ref[...], g_ref[...], sh_ref[...], sc_ref[...]).astype(BF16)
    q = jnp.dot(u, wq_ref[...], preferred_element_type=F32) * (HEAD_DIM_A ** -0.5)
    k = jnp.dot(u, wk_ref[...], preferred_element_type=F32)
    v = jnp.dot(u, wv_ref[...], preferred_element_type=F32)
    sm = jnp.dot(u, ws_ref[...], preferred_element_type=F32) + bs_ref[...]
    lane = _iota((1, LANES), 1)
    logf = jnp.where(lane < N_HEADS_A, -_softplus(-sm[:, :LANES]), 0.0)
    dt = jnp.where(lane < SSD_HEADS, _softplus(sm[:, LANES:]), 0.0)
    if not folded:
        q_ref, kb_ref, vb_ref, k32_ref, v32_ref, lf_ref, dt_ref = refs
        q_ref[...] = q.astype(BF16)
        kb_ref[...] = k.astype(BF16)
        vb_ref[...] = v.astype(BF16)
    else:
        qe_ref, qo_ref, ke_ref, ko_ref, ve_ref, vo_ref, k32_ref, v32_ref, lf_ref, dt_ref, carry_sc = refs
        tm = x_ref.shape[0]

        @pl.when(pl.program_id(0) == 0)
        def _():
            carry_sc[...] = jnp.zeros(carry_sc.shape, F32)

        lower = (_iota((tm, tm), 1) <= _iota((tm, tm), 0)).astype(BF16)
        f_cum = _dot01_left(lower, logf) + carry_sc[0:1, :]
        carry_sc[...] = jnp.broadcast_to(f_cum[tm - 1:tm, :], carry_sc.shape)
        hi, mid, lo = _split3(-f_cum)
        packed = (hi.astype(F32) + pltpu.roll(mid.astype(F32), N_HEADS_A, 1)
                  + pltpu.roll(lo.astype(F32), 2 * N_HEADS_A, 1)).astype(BF16)
        sel_even, sel_odd = _fold_selectors()
        fold_even = jnp.dot(packed, sel_even, preferred_element_type=F32)
        fold_odd = jnp.dot(packed, sel_odd, preferred_element_type=F32)
        lip = _iota((1, W_ATT), 1) % LANES
        in_even, in_odd = lip < HEAD_DIM_A, lip >= HEAD_DIM_A
        ones_even = ((lip >= HEAD_DIM_A) & (lip < HEAD_DIM_A + N_FOLD)).astype(F32)
        ones_odd = (lip < N_FOLD).astype(F32)
        qe_ref[...] = jnp.where(in_even, q, ones_even).astype(BF16)
        qo_ref[...] = jnp.where(in_odd, q, ones_odd).astype(BF16)
        ke_ref[...] = jnp.where(in_even, k, fold_even).astype(BF16)
        ko_ref[...] = jnp.where(in_odd, k, fold_odd).astype(BF16)
        ve_ref[...] = jnp.where(in_even, v, (lip == HEAD_DIM_A).astype(F32)).astype(BF16)
        vo_ref[...] = jnp.where(in_odd, v, (lip == 0).astype(F32)).astype(BF16)
    k32_ref[...] = k
    v32_ref[...] = v
    lf_ref[...] = logf[:, :N_HEADS_A]
    dt_ref[...] = dt


def _inproj_a(x, mod, g, wq, wk, wv, ws, bs, tm, folded):
    rows = x.shape[0]
    sd = jax.ShapeDtypeStruct
    n_bf = 6 if folded else 3
    return pl.pallas_call(
        functools.partial(_inproj_a_kernel, folded=folded),
        grid=(rows // tm,),
        in_specs=[_row_spec(tm, D_MODEL), _mod_spec(mod, tm, 3), _mod_spec(mod, tm, 4),
                  _resident((1, D_MODEL)),
                  _resident((D_MODEL, W_ATT)), _resident((D_MODEL, W_ATT)), _resident((D_MODEL, W_ATT)),
                  _resident((D_MODEL, 2 * LANES)), _resident((1, 2 * LANES))],
        out_specs=[_row_spec(tm, W_ATT)] * (n_bf + 2) + [_row_spec(tm, N_HEADS_A), _row_spec(tm, LANES)],
        out_shape=[sd((rows, W_ATT), BF16)] * n_bf + [sd((rows, W_ATT), F32)] * 2
                  + [sd((rows, N_HEADS_A), F32), sd((rows, LANES), F32)],
        scratch_shapes=[pltpu.VMEM((8, LANES), F32)] if folded else [],
        compiler_params=_params(1),
        name="inproj_qkv_folded" if folded else "inproj_qkv",
    )(x, mod, mod, g, wq, wk, wv, ws, bs)


CONV_TAIL = CONV_W - 1
CONV_FIRST = CONV_PAD - CONV_TAIL
CONV_COL_CHUNK = 512


def _causal_conv_silu(cbuf_sc, x, cw, cb, cols, t):
    cbuf_sc[CONV_PAD:CONV_PAD + t, cols] = x
    conv = cb + cw[0:1, :] * cbuf_sc[CONV_FIRST:CONV_FIRST + t, cols]
    for i in range(1, CONV_W):
        conv = conv + cw[i:i + 1, :] * cbuf_sc[CONV_FIRST + i:CONV_FIRST + i + t, cols]
    tail = cbuf_sc[t + CONV_FIRST:t + CONV_PAD, cols]
    cbuf_sc[CONV_FIRST:CONV_PAD, cols] = tail
    return _silu(conv), tail


def _inproj_b_kernel(x_ref, sh_ref, sc_ref, g_ref, wz_ref, wx_ref, wga_ref, wgs_ref, *refs, fused_conv):
    u = _modulated_norm(x_ref[...], g_ref[...], sh_ref[...], sc_ref[...]).astype(BF16)
    if not fused_conv:
        z_ref, xbc_ref, ga_ref, gs_ref = refs
        xbc_ref[...] = jnp.dot(u, wx_ref[...], preferred_element_type=F32)
    else:
        cst_ref, cw_ref, cb_ref, z_ref, xs_ref, bc_ref, ga_ref, gs_ref, cout_ref, cbuf_sc = refs
        tm = x_ref.shape[0]
        step = pl.program_id(0)

        @pl.when(step == 0)
        def _():
            cbuf_sc[CONV_FIRST:CONV_PAD, :] = cst_ref[...]

        for c0 in range(0, CONV_DIM, CONV_COL_CHUNK):
            cols = slice(c0, c0 + CONV_COL_CHUNK)
            xbc = jnp.dot(u, wx_ref[:, cols], preferred_element_type=F32)
            xc, tail = _causal_conv_silu(cbuf_sc, xbc, cw_ref[:, cols], cb_ref[:, cols], cols, tm)
            if c0 < D_INNER:
                xs_ref[:, cols] = xc.astype(BF16)
            else:
                bc_ref[:, c0 - D_INNER:c0 - D_INNER + CONV_COL_CHUNK] = xc.astype(BF16)
            cout_ref[:, cols] = tail

    z_ref[...] = jnp.dot(u, wz_ref[...], preferred_element_type=F32).astype(BF16)
    ga_ref[...] = jnp.dot(u, wga_ref[...], preferred_element_type=F32).astype(BF16)
    gs_ref[...] = jnp.dot(u, wgs_ref[...], preferred_element_type=F32).astype(BF16)


def _inproj_b(x, mod, g, wz, wx, wga, wgs, tm, conv=None):
    rows = x.shape[0]
    sd = jax.ShapeDtypeStruct
    fused = conv is not None
    in_specs = [_row_spec(tm, D_MODEL), _mod_spec(mod, tm, 3), _mod_spec(mod, tm, 4),
                _resident((1, D_MODEL)),
                _resident((D_MODEL, D_INNER)), _resident((D_MODEL, CONV_DIM)),
                _resident((D_MODEL, D_MODEL)), _resident((D_MODEL, D_MODEL))]
    gates = [_row_spec(tm, D_MODEL), _row_spec(tm, D_MODEL)]
    gate_shapes = [sd((rows, D_MODEL), BF16), sd((rows, D_MODEL), BF16)]
    if fused:
        in_specs += [_resident((CONV_TAIL, CONV_DIM)), _resident((CONV_W, CONV_DIM)), _resident((1, CONV_DIM))]
        out_specs = [_row_spec(tm, D_INNER), _row_spec(tm, D_INNER), _row_spec(tm, CONV_DIM - D_INNER)] + gates \
            + [pl.BlockSpec((CONV_TAIL, CONV_DIM), lambda i: (0, 0))]
        out_shape = [sd((rows, D_INNER), BF16), sd((rows, D_INNER), BF16), sd((rows, CONV_DIM - D_INNER), BF16)] \
            + gate_shapes + [sd((CONV_TAIL, CONV_DIM), F32)]
        scratch = [pltpu.VMEM((CONV_PAD + tm, CONV_DIM), F32)]
    else:
        out_specs = [_row_spec(tm, D_INNER), _row_spec(tm, CONV_DIM)] + gates
        out_shape = [sd((rows, D_INNER), BF16), sd((rows, CONV_DIM), F32)] + gate_shapes
        scratch = []
    return pl.pallas_call(
        functools.partial(_inproj_b_kernel, fused_conv=fused),
        grid=(rows // tm,),
        in_specs=in_specs,
        out_specs=out_specs,
        out_shape=out_shape,
        scratch_shapes=scratch,
        compiler_params=_params(1),
        name="inproj_ssd_conv" if fused else "inproj_ssd",
    )(x, mod, mod, g, wz, wx, wga, wgs, *(conv or ()))


def _merge_kernel(x_ref, o_ref, y_ref, z_ref, ga_ref, gs_ref, g2_ref, gssd_ref, wa_ref, ws_ref, wo_ref,
                  out_ref):
    yg = y_ref[...].astype(F32) * _silu(z_ref[...].astype(F32))
    ms = jnp.mean(yg * yg, axis=-1, keepdims=True)
    yn = (yg * lax.rsqrt(ms + EPS) * gssd_ref[...]).astype(BF16)
    o_s = jnp.dot(yn, ws_ref[...], preferred_element_type=F32)
    o_a = jnp.dot(o_ref[...], wa_ref[...], preferred_element_type=F32)
    merged = (jax.nn.sigmoid(ga_ref[...].astype(F32)) * o_a
              + jax.nn.sigmoid(gs_ref[...].astype(F32)) * o_s)
    m = jnp.dot(merged.astype(BF16), wo_ref[...], preferred_element_type=F32)
    out_ref[...] = x_ref[...] + (1.0 + g2_ref[...]) * m


def _merge(x, o, y, z, ga, gs, mod, g_ssd, wa, ws, wo, tm):
    rows = x.shape[0]
    return pl.pallas_call(
        _merge_kernel,
        grid=(rows // tm,),
        in_specs=[_row_spec(tm, D_MODEL), _row_spec(tm, W_ATT), _row_spec(tm, D_INNER),
                  _row_spec(tm, D_INNER), _row_spec(tm, D_MODEL), _row_spec(tm, D_MODEL),
                  _mod_spec(mod, tm, 5), _resident((1, D_INNER)),
                  _resident((W_ATT, D_MODEL)), _resident((D_INNER, D_MODEL)), _resident((D_MODEL, D_MODEL))],
        out_specs=_row_spec(tm, D_MODEL),
        out_shape=jax.ShapeDtypeStruct((rows, D_MODEL), F32),
        compiler_params=_params(1),
        name="merge",
    )(x, o, y, z, ga, gs, mod, g_ssd, wa, ws, wo)


def _fcum_sample_kernel(past_ref, new_ref, fp_ref, fn_ref):
    p = past_ref.shape[1]
    upper_p = (_iota((p, p), 0) <= _iota((p, p), 1)).astype(BF16)
    upper_n = (_iota((LANES, LANES), 0) <= _iota((LANES, LANES), 1)).astype(BF16)
    past = past_ref[...]
    fp_ref[...] = _dot01_right(past, upper_p)
    total = _dot01_right(past, jnp.ones((p, LANES), BF16))
    fn_ref[...] = total + _dot01_right(new_ref[...], upper_n)


def _fcum_sample(past, new):
    rows, p = past.shape
    return pl.pallas_call(
        _fcum_sample_kernel,
        out_shape=[jax.ShapeDtypeStruct((rows, p), F32), jax.ShapeDtypeStruct((rows, LANES), F32)],
        compiler_params=pltpu.CompilerParams(vmem_limit_bytes=VMEM_LIMIT_BYTES),
        name="fcum_sample",
    )(past, new)


def _pair_split(x):
    lane = _iota((1, LANES), 1)
    keep0 = (lane < HEAD_DIM_A).astype(F32).astype(x.dtype)
    keep1 = (lane >= HEAD_DIM_A).astype(F32).astype(x.dtype)
    return x * keep0, x * keep1


def _pair_normalize(acc_even, acc_odd):
    lane = _iota((1, LANES), 1)
    return jnp.where(lane < HEAD_DIM_A,
                     acc_even / acc_even[:, HEAD_DIM_A:HEAD_DIM_A + 1], acc_odd / acc_odd[:, 0:1])


def _attn_prompt_kernel(qe_ref, qo_ref, ke_ref, ko_ref, ve_ref, vo_ref, o_ref, m_sc, acc_sc):
    qi = pl.program_id(1)
    tq = qe_ref.shape[0]
    tk = ATT_KEY_BLOCK
    blocks_per_q = tq // tk
    q_refs, k_refs, v_refs = (qe_ref, qo_ref), (ke_ref, ko_ref), (ve_ref, vo_ref)
    m_sc[...] = jnp.full(m_sc.shape, -jnp.inf, F32)
    acc_sc[...] = jnp.zeros(acc_sc.shape, F32)

    def unit(r0, nrows, j, key_start):
        off = pl.multiple_of(j * tk, tk)
        rows = slice(r0, r0 + nrows)
        scores = []
        for hh in range(2):
            s = lax.dot_general(q_refs[hh][rows, :], k_refs[hh][pl.ds(off, tk), :], _NT,
                                preferred_element_type=F32)
            if key_start is not None:
                keep = (key_start + _iota((nrows, tk), 1)) <= (r0 + _iota((nrows, tk), 0))
                s = jnp.where(keep, s, -jnp.inf)
            scores.append(s)
        probs = []
        for hh in range(2):
            m_prev = m_sc[hh, rows, :]
            m_new = jnp.maximum(m_prev, jnp.max(scores[hh], axis=1, keepdims=True))
            p = jnp.exp(scores[hh] - jnp.concatenate([m_new] * (tk // LANES), axis=1))
            probs.append((p.astype(BF16), jnp.exp(m_prev - m_new)))
            m_sc[hh, rows, :] = m_new
        for hh in range(2):
            p, alpha = probs[hh]
            pv = jnp.dot(p, v_refs[hh][pl.ds(off, tk), :], preferred_element_type=F32)
            acc_sc[hh, rows, :] = alpha * acc_sc[hh, rows, :] + pv

    def body(jj, carry):
        for d in range(blocks_per_q):
            unit(0, tq, jj * blocks_per_q + d, None)
        return carry

    first_diag = qi * blocks_per_q
    lax.fori_loop(0, qi, body, 0)
    for d in range(blocks_per_q):
        unit(d * tk, tq - d * tk, first_diag + d, d * tk)

    o_ref[...] = _pair_normalize(acc_sc[0], acc_sc[1]).astype(o_ref.dtype)


def _attn_prompt(qe, qo, ke, ko, ve, vo):
    s = qe.shape[0]
    tq = ATT_QUERY_BLOCK
    n_pairs = W_ATT // LANES
    q_spec = pl.BlockSpec((tq, LANES), lambda p, i: (i, p))
    kv_spec = pl.BlockSpec((s, LANES), lambda p, i: (0, p))
    return pl.pallas_call(
        _attn_prompt_kernel,
        grid=(n_pairs, s // tq),
        in_specs=[q_spec, q_spec, kv_spec, kv_spec, kv_spec, kv_spec],
        out_specs=q_spec,
        out_shape=jax.ShapeDtypeStruct((s, W_ATT), BF16),
        scratch_shapes=[pltpu.VMEM((2, tq, LANES), F32), pltpu.VMEM((2, tq, LANES), F32)],
        compiler_params=_params(2),
        name="attn_prompt",
    )(qe, qo, ke, ko, ve, vo)


def _attn_sample_kernel(q_ref, kn_ref, vn_ref, kc_ref, vc_ref, fp_ref, fn_ref, o_ref):
    n = q_ref.shape[0]
    lane = _iota((1, LANES), 1)
    q_stack = jnp.concatenate(_pair_split(q_ref[...]), axis=0)
    f_past = jnp.concatenate([jnp.broadcast_to(fp_ref[hh:hh + 1, :], (n, fp_ref.shape[1])) for hh in range(2)],
                             axis=0)
    f_new = jnp.concatenate([jnp.broadcast_to(fn_ref[hh:hh + 1, :n], (n, n)) for hh in range(2)], axis=0)
    s_past = lax.dot_general(q_stack, kc_ref[...].astype(BF16), _NT, preferred_element_type=F32) - f_past
    s_new = lax.dot_general(q_stack, kn_ref[...], _NT, preferred_element_type=F32) - f_new
    query = _iota((2 * n, n), 0) % n
    s_new = jnp.where(_iota((2 * n, n), 1) <= query, s_new, -jnp.inf)
    m = jnp.maximum(jnp.max(s_past, axis=1, keepdims=True), jnp.max(s_new, axis=1, keepdims=True))
    p_past = jnp.exp(s_past - m)
    p_new = jnp.exp(s_new - m)
    denom = jnp.sum(p_past, axis=1, keepdims=True) + jnp.sum(p_new, axis=1, keepdims=True)
    acc = (jnp.dot(p_past.astype(BF16), vc_ref[...].astype(BF16), preferred_element_type=F32)
           + jnp.dot(p_new.astype(BF16), vn_ref[...], preferred_element_type=F32)) / denom
    o_ref[...] = jnp.where(lane < HEAD_DIM_A, acc[:n], acc[n:]).astype(o_ref.dtype)


def _attn_sample(q, kn, vn, cache_k, cache_v, f_past, f_new, n_batch, n_new):
    past = cache_k.shape[1]
    n_pairs = W_ATT // LANES
    tok = pl.BlockSpec((n_new, LANES), lambda b, p: (b, p))
    cache = pl.BlockSpec((None, past, LANES), lambda b, p: (b, 0, p))
    return pl.pallas_call(
        _attn_sample_kernel,
        grid=(n_batch, n_pairs),
        in_specs=[tok, tok, tok, cache, cache,
                  pl.BlockSpec((None, None, 2, past), lambda b, p: (b, p, 0, 0)),
                  pl.BlockSpec((None, None, 2, LANES), lambda b, p: (b, p, 0, 0))],
        out_specs=tok,
        out_shape=jax.ShapeDtypeStruct((n_batch * n_new, W_ATT), BF16),
        compiler_params=_params(2),
        name="attn_sample",
    )(q, kn, vn, cache_k, cache_v, f_past, f_new)


def _ssd_chunk(xs, b_all, c_all, dt, alog_ref, dsk_ref, y_ref, ht_sc):
    t = xs.shape[0]
    hp_group = D_INNER // SSD_GROUPS
    heads_per_group = SSD_HEADS // SSD_GROUPS
    lane = _iota((1, LANES), 1)

    a_row = -jnp.exp(alog_ref[...])
    lower = (_iota((t, t), 1) <= _iota((t, t), 0)).astype(BF16)
    acs = _dot01_left(lower, dt * a_row)
    acs_end = acs[t - 1:t, :]
    compact = jnp.concatenate([jnp.exp(acs_end - acs) * dt, dt, jnp.exp(acs)], axis=0)
    expand = ((_iota((LANES, D_INNER), 1) // SSD_HEADDIM) == _iota((LANES, D_INNER), 0)).astype(BF16)
    wide = _dot01_right(compact, expand, pieces=2)
    xs_state = (xs * wide[0:t]).astype(BF16)
    xs_dt = (xs * wide[t:2 * t]).astype(BF16)
    e_acs = wide[2 * t:3 * t]
    chunk_decay = e_acs[t - 1:t, :]

    eye = (_iota((LANES, LANES), 0) == _iota((LANES, LANES), 1)).astype(BF16)
    acs_t = _transpose01(eye, acs)
    causal = _iota((t, t), 1) <= _iota((t, t), 0)

    for g in range(SSD_GROUPS):
        b_g = b_all[:, g * SSD_STATE:(g + 1) * SSD_STATE]
        c_g = c_all[:, g * SSD_STATE:(g + 1) * SSD_STATE]
        cols = slice(g * hp_group, (g + 1) * hp_group)
        ht_g = ht_sc[:, cols]
        y_off = jnp.dot(c_g, ht_g.astype(BF16), preferred_element_type=F32) * e_acs[:, cols]
        states = lax.dot_general(b_g, xs_state[:, cols], _TN, preferred_element_type=F32)
        ht_sc[:, cols] = ht_g * chunk_decay[:, cols] + states
        cb = lax.dot_general(c_g, b_g, _NT, preferred_element_type=F32)
        for pair in range(heads_per_group // 2):
            h0 = g * heads_per_group + 2 * pair
            lo = h0 * SSD_HEADDIM
            x_pair = xs_dt[:, lo:lo + LANES]
            y_heads = []
            for h in (h0, h0 + 1):
                seg = acs[:, h:h + 1] - acs_t[h:h + 1, :]
                w = (cb * jnp.exp(jnp.where(causal, seg, -jnp.inf))).astype(BF16)
                y_heads.append(jnp.dot(w, x_pair, preferred_element_type=F32))
            y_pair = (y_off[:, lo - g * hp_group:lo - g * hp_group + LANES]
                      + jnp.where(lane < SSD_HEADDIM, y_heads[0], y_heads[1])
                      + dsk_ref[:, lo:lo + LANES] * xs[:, lo:lo + LANES])
            y_ref[:, lo:lo + LANES] = y_pair.astype(y_ref.dtype)


def _load_state_transposed(h0_ref, ht_sc):
    for b in range(D_INNER // LANES):
        ht_sc[:, b * LANES:(b + 1) * LANES] = h0_ref[b * LANES:(b + 1) * LANES, :].T


def _store_state(ht_sc, hout_ref):
    for b in range(D_INNER // LANES):
        hout_ref[b * LANES:(b + 1) * LANES, :] = ht_sc[:, b * LANES:(b + 1) * LANES].T


def _ssd_conv_kernel(xbc_ref, dt_ref, cst_ref, h0_ref, cw_ref, cb_ref, alog_ref, dsk_ref,
                     y_ref, hout_ref, cout_ref, ht_sc, cbuf_sc):
    c = pl.program_id(1)
    t = xbc_ref.shape[0]

    @pl.when(c == 0)
    def _():
        cbuf_sc[CONV_FIRST:CONV_PAD, :] = cst_ref[...]
        _load_state_transposed(h0_ref, ht_sc)

    xc, tail = _causal_conv_silu(cbuf_sc, xbc_ref[...], cw_ref[...], cb_ref[...], slice(None), t)
    _ssd_chunk(xc[:, :D_INNER], xc[:, D_INNER:D_INNER + SSD_GROUPS * SSD_STATE].astype(BF16),
               xc[:, D_INNER + SSD_GROUPS * SSD_STATE:].astype(BF16), dt_ref[...], alog_ref, dsk_ref, y_ref, ht_sc)

    @pl.when(c == pl.num_programs(1) - 1)
    def _():
        cout_ref[...] = tail
        _store_state(ht_sc, hout_ref)


def _ssd_kernel(xs_ref, bc_ref, dt_ref, h0_ref, alog_ref, dsk_ref, y_ref, hout_ref, ht_sc):
    c = pl.program_id(1)

    @pl.when(c == 0)
    def _():
        _load_state_transposed(h0_ref, ht_sc)

    gn = SSD_GROUPS * SSD_STATE
    _ssd_chunk(xs_ref[...].astype(F32), bc_ref[:, :gn], bc_ref[:, gn:], dt_ref[...], alog_ref, dsk_ref,
               y_ref, ht_sc)

    @pl.when(c == pl.num_programs(1) - 1)
    def _():
        _store_state(ht_sc, hout_ref)


def _ssd(x_in, dt, ssm_state, a_log_row, d_skip_row, n_batch, seq, chunk, conv=None):
    assert seq % chunk == 0 and chunk >= CONV_TAIL
    nc = seq // chunk
    rows = lambda b, c: (b * nc + c, 0)
    per_batch3 = lambda b, c: (b, 0, 0)
    const2 = lambda b, c: (0, 0)
    sd = jax.ShapeDtypeStruct
    state_spec = pl.BlockSpec((None, D_INNER, SSD_STATE), per_batch3)
    tail_specs = [pl.BlockSpec((1, LANES), const2), pl.BlockSpec((1, D_INNER), const2)]
    y_spec = pl.BlockSpec((chunk, D_INNER), rows)
    y_shape = sd((n_batch * seq, D_INNER), BF16)
    state_shape = sd((n_batch, D_INNER, SSD_STATE), F32)
    state_scratch = pltpu.VMEM((SSD_STATE, D_INNER), F32)
    if conv is None:
        xs, bc = x_in
        y, h_new = pl.pallas_call(
            _ssd_kernel,
            grid=(n_batch, nc),
            in_specs=[pl.BlockSpec((chunk, D_INNER), rows), pl.BlockSpec((chunk, CONV_DIM - D_INNER), rows),
                      pl.BlockSpec((chunk, LANES), rows), state_spec] + tail_specs,
            out_specs=[y_spec, state_spec],
            out_shape=[y_shape, state_shape],
            scratch_shapes=[state_scratch],
            compiler_params=_params(2),
            name="ssd_scan",
        )(xs, bc, dt, ssm_state, a_log_row, d_skip_row)
        return y, h_new, None
    conv_state, conv_w, conv_b = conv
    conv_spec = pl.BlockSpec((None, CONV_TAIL, CONV_DIM), per_batch3)
    return pl.pallas_call(
        _ssd_conv_kernel,
        grid=(n_batch, nc),
        in_specs=[pl.BlockSpec((chunk, CONV_DIM), rows), pl.BlockSpec((chunk, LANES), rows),
                  conv_spec, state_spec,
                  pl.BlockSpec((CONV_W, CONV_DIM), const2), pl.BlockSpec((1, CONV_DIM), const2)] + tail_specs,
        out_specs=[y_spec, state_spec, conv_spec],
        out_shape=[y_shape, state_shape, sd((n_batch, CONV_TAIL, CONV_DIM), F32)],
        scratch_shapes=[state_scratch, pltpu.VMEM((CONV_PAD + chunk, CONV_DIM), F32)],
        compiler_params=_params(2),
        name="ssd_conv_scan",
    )(x_in, dt, conv_state, ssm_state, conv_w, conv_b, a_log_row, d_skip_row)


def _pad_lanes(row, width=LANES):
    return jnp.pad(row, ((0, 0), (0, width - row.shape[1])))


def _layer_weights(w_in, b_f, dt_bias, a_log, d_skip, prm):
    o = 0
    pieces = {}
    for name, width in (("q", W_ATT), ("k", W_ATT), ("v", W_ATT), ("f", N_HEADS_A), ("z", D_INNER),
                        ("xbc", CONV_DIM), ("dt", SSD_HEADS), ("ga", D_MODEL), ("gs", D_MODEL)):
        pieces[name] = w_in[:, o:o + width]
        o += width
    w = {k: pieces[k].astype(BF16) for k in ("q", "k", "v", "z", "xbc", "ga", "gs")}
    w["small"] = jnp.concatenate([_pad_lanes(pieces["f"]), _pad_lanes(pieces["dt"])], axis=1).astype(BF16)
    w["small_bias"] = jnp.concatenate([_pad_lanes(b_f[None, :]), _pad_lanes(dt_bias[None, :])], axis=1)
    w["a_log_row"] = _pad_lanes(a_log[None, :])
    w["d_skip_row"] = jnp.repeat(d_skip, SSD_HEADDIM)[None, :]
    for k, v in prm.items():
        w[k] = v.astype(BF16) if v.ndim == 2 and v.shape[0] >= D_MODEL else v
    return w


def _mixer_and_ffn(x, mod, w, tm, g_final, attend, single_sequence, conv_state, ssm_state, n_batch, seq, chunk):
    x, _ = _ffn(x, mod, 0, w["g_ffn1"], w["w1_ffn1"], w["w3_ffn1"], w["w2_ffn1"], g_final, tm, False)
    *att_ops, k32, v32, logf, dt = _inproj_a(x, mod, w["g_mix"], w["q"], w["k"], w["v"],
                                             w["small"], w["small_bias"], tm, single_sequence)
    o = attend(att_ops, logf)
    if single_sequence:
        z, xs, bc, ga, gs, conv_new = _inproj_b(x, mod, w["g_mix"], w["z"], w["xbc"], w["ga"], w["gs"], tm,
                                                (conv_state[0], w["conv_w"], w["conv_b"]))
        y, ssm_new, _ = _ssd((xs, bc), dt, ssm_state, w["a_log_row"], w["d_skip_row"], n_batch, seq, chunk)
        conv_new = conv_new[None]
    else:
        z, xbc, ga, gs = _inproj_b(x, mod, w["g_mix"], w["z"], w["xbc"], w["ga"], w["gs"], tm)
        y, ssm_new, conv_new = _ssd(xbc, dt, ssm_state, w["a_log_row"], w["d_skip_row"], n_batch, seq, chunk,
                                    (conv_state, w["conv_w"], w["conv_b"]))
    x = _merge(x, o, y, z, ga, gs, mod, w["g_ssd"], w["w_a"], w["w_s"], w["w_out"], tm)
    x, y_final = _ffn(x, mod, 6, w["g_ffn2"], w["w1_ffn2"], w["w3_ffn2"], w["w2_ffn2"], g_final, tm, True)
    return x, y_final, (k32, v32, logf, ssm_new, conv_new)


def kernel(x_prompt, x_sample, c_prompt, c_sample, cache_k, cache_v, cache_logf, state_ssm, state_conv,
           w_ada, b_ada, g_ffn1, w1_ffn1, w3_ffn1, w2_ffn1, g_mix, w_in, b_f, conv_w, conv_b,
           dt_bias, a_log, d_skip, g_ssd, w_a, w_s, w_out, g_ffn2, w1_ffn2, w3_ffn2, w2_ffn2, g_final):
    depth = w_ada.shape[0]
    bp, seq_p, _ = x_prompt.shape
    bs, seq_s, _ = x_sample.shape
    past = cache_k.shape[2]
    assert bp == 1
    n_pairs = W_ATT // LANES

    xp = x_prompt.reshape(bp * seq_p, D_MODEL)
    xs = x_sample.reshape(bs * seq_s, D_MODEL)
    c_rows = bs + bp
    c_all = jnp.pad(jnp.concatenate([c_sample, c_prompt], axis=0), ((0, -c_rows % 8), (0, 0)))
    gf = g_final[None, :]

    outs_p = [[] for _ in range(5)]
    outs_s = [[] for _ in range(5)]
    yp = ys = None
    for l in range(depth):
        prm = {"g_ffn1": g_ffn1[l][None, :], "w1_ffn1": w1_ffn1[l], "w3_ffn1": w3_ffn1[l], "w2_ffn1": w2_ffn1[l],
               "g_mix": g_mix[l][None, :], "conv_w": conv_w[l], "conv_b": conv_b[l][None, :],
               "g_ssd": g_ssd[l][None, :], "w_a": w_a[l], "w_s": w_s[l], "w_out": w_out[l],
               "g_ffn2": g_ffn2[l][None, :], "w1_ffn2": w1_ffn2[l], "w3_ffn2": w3_ffn2[l], "w2_ffn2": w2_ffn2[l]}
        w = _layer_weights(w_in[l], b_f[l], dt_bias[l], a_log[l], d_skip[l], prm)

        mod = _ada(c_all, w_ada[l], b_ada[l])
        mod_s = jnp.repeat(mod[:bs], seq_s, axis=0)
        mod_p = mod[bs:bs + 1]

        def attend_prompt(att_ops, logf):
            return _attn_prompt(*att_ops)

        def attend_sample(att_ops, logf):
            q, kb, vb = att_ops
            past_rows = cache_logf[l].transpose(0, 2, 1).reshape(bs * N_HEADS_A, past)
            new_rows = logf.reshape(bs, seq_s, N_HEADS_A).transpose(0, 2, 1).reshape(bs * N_HEADS_A, seq_s)
            f_past, f_new = _fcum_sample(past_rows, _pad_lanes(new_rows))
            return _attn_sample(q, kb, vb,
                                cache_k[l].reshape(bs, past, W_ATT), cache_v[l].reshape(bs, past, W_ATT),
                                f_past.reshape(bs, n_pairs, 2, past), f_new.reshape(bs, n_pairs, 2, LANES),
                                bs, seq_s)

        zero_conv = jnp.zeros((bp, CONV_W - 1, CONV_DIM), F32)
        zero_ssm = jnp.zeros((bp, D_INNER, SSD_STATE), F32)
        xp, yp, new_p = _mixer_and_ffn(xp, mod_p, w, ROW_TILE, gf, attend_prompt, True, zero_conv, zero_ssm,
                                       bp, seq_p, SSD_CHUNK_PROMPT)
        xs, ys, new_s = _mixer_and_ffn(xs, mod_s, w, bs * seq_s, gf, attend_sample, False, state_conv[l],
                                       state_ssm[l].reshape(bs, D_INNER, SSD_STATE), bs, seq_s, seq_s)
        for dst, new, nb, sq in ((outs_p, new_p, bp, seq_p), (outs_s, new_s, bs, seq_s)):
            k32, v32, logf, ssm_new, conv_new = new
            dst[0].append(k32.reshape(nb, sq, N_HEADS_A, HEAD_DIM_A))
            dst[1].append(v32.reshape(nb, sq, N_HEADS_A, HEAD_DIM_A))
            dst[2].append(logf.reshape(nb, sq, N_HEADS_A))
            dst[3].append(ssm_new.reshape(nb, SSD_HEADS, SSD_HEADDIM, SSD_STATE))
            dst[4].append(conv_new)

    y_prompt = yp.reshape(bp, seq_p, D_MODEL)
    y_sample = ys.reshape(bs, seq_s, D_MODEL)
    return (y_prompt, y_sample, *[jnp.stack(o) for o in outs_p], *[jnp.stack(o) for o in outs_s])
```

```python
import functools

import jax
import jax.numpy as jnp
from jax import lax
from jax.experimental import pallas as pl
from jax.experimental.pallas import tpu as pltpu

F32 = jnp.float32
BF16 = jnp.bfloat16

EPS = 1e-6
D_MODEL = 1024
N_HEADS_A = 16
HEAD_DIM_A = 64
W_ATT = N_HEADS_A * HEAD_DIM_A
D_INNER = 2 * D_MODEL
SSD_HEADDIM = 64
SSD_HEADS = D_INNER // SSD_HEADDIM
SSD_STATE = 128
SSD_GROUPS = 4
CONV_W = 4
CONV_DIM = D_INNER + 2 * SSD_GROUPS * SSD_STATE
D_FF = 2816
N_MOD = 9

LANES = 128
VMEM_LIMIT_BYTES = 56 * 1024 * 1024

ROW_TILE = 512
FF_CHUNK = 1408
ATT_QUERY_BLOCK = 1024
ATT_KEY_BLOCK = 512
ATT_BLOCKS_PER_TRIP = 4
SSD_CHUNK_PROMPT = 128
CONV_PAD = 8

_NT = (((1,), (1,)), ((), ()))
_TN = (((0,), (0,)), ((), ()))


def _resident(shape):
    nd = len(shape)
    return pl.BlockSpec(shape, lambda *_: (0,) * nd, pipeline_mode=pl.Buffered(1))


def _params(n_axes):
    return pltpu.CompilerParams(dimension_semantics=("arbitrary",) * n_axes,
                                vmem_limit_bytes=VMEM_LIMIT_BYTES)


def _silu(x):
    return x * jax.nn.sigmoid(x)


def _softplus(x):
    return jnp.maximum(x, 0.0) + jnp.log1p(jnp.exp(-jnp.abs(x)))


def _split3(x, pieces=3):
    out = []
    for _ in range(pieces - 1):
        hi = x.astype(BF16)
        out.append(hi)
        x = x - hi.astype(F32)
    out.append(x.astype(BF16))
    return out


def _dot01_right(x, u01, pieces=3):
    acc = None
    for piece in _split3(x, pieces):
        t = jnp.dot(piece, u01, preferred_element_type=F32)
        acc = t if acc is None else acc + t
    return acc


def _dot01_left(l01, x):
    acc = None
    for piece in _split3(x):
        t = jnp.dot(l01, piece, preferred_element_type=F32)
        acc = t if acc is None else acc + t
    return acc


def _transpose01(eye, x):
    acc = None
    for piece in _split3(x):
        t = lax.dot_general(eye, piece, _NT, preferred_element_type=F32)
        acc = t if acc is None else acc + t
    return acc


def _iota(shape, dim):
    return lax.broadcasted_iota(jnp.int32, shape, dim)


def _modulated_norm(x, g, shift, scale):
    ms = jnp.mean(x * x, axis=-1, keepdims=True)
    h = x * lax.rsqrt(ms + EPS) * g
    return h * (1.0 + scale) + shift


def _ada_kernel(c_ref, w_ref, b_ref, o_ref):
    c = _silu(c_ref[...]).astype(BF16)
    o_ref[...] = jnp.dot(c, w_ref[...].astype(BF16), preferred_element_type=F32) + b_ref[...]


def _ada(c_all, w_ada, b_ada):
    rows = c_all.shape[0]
    n = w_ada.shape[1]
    tn = 1536
    return pl.pallas_call(
        _ada_kernel,
        grid=(n // tn,),
        in_specs=[pl.BlockSpec((rows, D_MODEL), lambda j: (0, 0)),
                  pl.BlockSpec((D_MODEL, tn), lambda j: (0, j)),
                  pl.BlockSpec((1, tn), lambda j: (0, j))],
        out_specs=pl.BlockSpec((rows, tn), lambda j: (0, j)),
        out_shape=jax.ShapeDtypeStruct((rows, n), F32),
        compiler_params=_params(1),
        name="ada_mod",
    )(c_all, w_ada, b_ada.reshape(1, n))


def _mod_spec(mod, tm, k):
    if mod.shape[0] == 1:
        return pl.BlockSpec((1, D_MODEL), lambda i: (0, k))
    return pl.BlockSpec((tm, D_MODEL), lambda i: (i, k))


def _row_spec(tm, width):
    return pl.BlockSpec((tm, width), lambda i: (i, 0))


def _ffn_kernel(x_ref, sh_ref, sc_ref, ga_ref, g_ref, w1_ref, w3_ref, w2_ref, gf_ref, o_ref, *maybe_y_ref,
                final_norm):
    x = x_ref[...]
    h = _modulated_norm(x, g_ref[...], sh_ref[...], sc_ref[...]).astype(BF16)
    y = None
    for c0 in range(0, D_FF, FF_CHUNK):
        a = jnp.dot(h, w1_ref[:, c0:c0 + FF_CHUNK], preferred_element_type=F32)
        b = jnp.dot(h, w3_ref[:, c0:c0 + FF_CHUNK], preferred_element_type=F32)
        t = jnp.dot((_silu(a) * b).astype(BF16), w2_ref[c0:c0 + FF_CHUNK, :], preferred_element_type=F32)
        y = t if y is None else y + t
    out = x + 0.5 * (1.0 + ga_ref[...]) * y
    o_ref[...] = out
    if final_norm:
        (y_ref,) = maybe_y_ref
        ms = jnp.mean(out * out, axis=-1, keepdims=True)
        y_ref[...] = out * lax.rsqrt(ms + EPS) * gf_ref[...]


def _ffn(x, mod, k0, g, w1, w3, w2, g_final, tm, final_norm):
    rows = x.shape[0]
    n_out = 2 if final_norm else 1
    res = pl.pallas_call(
        functools.partial(_ffn_kernel, final_norm=final_norm),
        grid=(rows // tm,),
        in_specs=[_row_spec(tm, D_MODEL),
                  _mod_spec(mod, tm, k0), _mod_spec(mod, tm, k0 + 1), _mod_spec(mod, tm, k0 + 2),
                  _resident((1, D_MODEL)),
                  _resident((D_MODEL, D_FF)), _resident((D_MODEL, D_FF)), _resident((D_FF, D_MODEL)),
                  _resident((1, D_MODEL))],
        out_specs=[_row_spec(tm, D_MODEL)] * n_out,
        out_shape=[jax.ShapeDtypeStruct((rows, D_MODEL), F32)] * n_out,
        compiler_params=_params(1),
        name="ffn_final" if final_norm else "ffn",
    )(x, mod, mod, mod, g, w1, w3, w2, g_final)
    return (res[0], res[1]) if final_norm else (res[0], None)


N_FOLD = 3


def _fold_selectors():
    r = _iota((LANES, W_ATT), 0)
    j = _iota((LANES, W_ATT), 1)
    piece, head = r // N_HEADS_A, r % N_HEADS_A
    same_pair = (j // LANES) == (head // 2)
    valid = r < N_FOLD * N_HEADS_A
    lane = j % LANES
    even = valid & same_pair & (head % 2 == 0) & (lane == HEAD_DIM_A + piece)
    odd = valid & same_pair & (head % 2 == 1) & (lane == piece)
    return even.astype(BF16), odd.astype(BF16)


def _inproj_a_kernel(x_ref, sh_ref, sc_ref, g_ref, wq_ref, wk_ref, wv_ref, ws_ref, bs_ref, *refs, folded):
    u = _modulated_norm(x_ref[...], g_ref[...], sh_ref[...], sc_ref[...]).astype(BF16)
    q = jnp.dot(u, wq_ref[...], preferred_element_type=F32) * (HEAD_DIM_A ** -0.5)
    k = jnp.dot(u, wk_ref[...], preferred_element_type=F32)
    v = jnp.dot(u, wv_ref[...], preferred_element_type=F32)
    sm = jnp.dot(u, ws_ref[...], preferred_element_type=F32) + bs_ref[...]
    lane = _iota((1, LANES), 1)
    logf = jnp.where(lane < N_HEADS_A, -_softplus(-sm[:, :LANES]), 0.0)
    dt = jnp.where(lane < SSD_HEADS, _softplus(sm[:, LANES:]), 0.0)
    if not folded:
        q_ref, kb_ref, vb_ref, k32_ref, v32_ref, lf_ref, dt_ref = refs
        q_ref[...] = q.astype(BF16)
        kb_ref[...] = k.astype(BF16)
        vb_ref[...] = v.astype(BF16)
    else:
        qe_ref, qo_ref, ke_ref, ko_ref, ve_ref, vo_ref, k32_ref, v32_ref, lf_ref, dt_ref, carry_sc = refs
        tm = x_ref.shape[0]

        @pl.when(pl.program_id(0) == 0)
        def _():
            carry_sc[...] = jnp.zeros(carry_sc.shape, F32)

        lower = (_iota((tm, tm), 1) <= _iota((tm, tm), 0)).astype(BF16)
        f_cum = _dot01_left(lower, logf) + carry_sc[0:1, :]
        carry_sc[...] = jnp.broadcast_to(f_cum[tm - 1:tm, :], carry_sc.shape)
        hi, mid, lo = _split3(-f_cum)
        packed = (hi.astype(F32) + pltpu.roll(mid.astype(F32), N_HEADS_A, 1)
                  + pltpu.roll(lo.astype(F32), 2 * N_HEADS_A, 1)).astype(BF16)
        sel_even, sel_odd = _fold_selectors()
        fold_even = jnp.dot(packed, sel_even, preferred_element_type=F32)
        fold_odd = jnp.dot(packed, sel_odd, preferred_element_type=F32)
        lip = _iota((1, W_ATT), 1) % LANES
        in_even, in_odd = lip < HEAD_DIM_A, lip >= HEAD_DIM_A
        ones_even = ((lip >= HEAD_DIM_A) & (lip < HEAD_DIM_A + N_FOLD)).astype(F32)
        ones_odd = (lip < N_FOLD).astype(F32)
        qe_ref[...] = jnp.where(in_even, q, ones_even).astype(BF16)
        qo_ref[...] = jnp.where(in_odd, q, ones_odd).astype(BF16)
        ke_ref[...] = jnp.where(in_even, k, fold_even).astype(BF16)
        ko_ref[...] = jnp.where(in_odd, k, fold_odd).astype(BF16)
        ve_ref[...] = jnp.where(in_even, v, (lip == HEAD_DIM_A).astype(F32)).astype(BF16)
        vo_ref[...] = jnp.where(in_odd, v, (lip == 0).astype(F32)).astype(BF16)
    k32_ref[...] = k
    v32_ref[...] = v
    lf_ref[...] = logf[:, :N_HEADS_A]
    dt_ref[...] = dt


def _inproj_a(x, mod, g, wq, wk, wv, ws, bs, tm, folded):
    rows = x.shape[0]
    sd = jax.ShapeDtypeStruct
    n_bf = 6 if folded else 3
    return pl.pallas_call(
        functools.partial(_inproj_a_kernel, folded=folded),
        grid=(rows // tm,),
        in_specs=[_row_spec(tm, D_MODEL), _mod_spec(mod, tm, 3), _mod_spec(mod, tm, 4),
                  _resident((1, D_MODEL)),
                  _resident((D_MODEL, W_ATT)), _resident((D_MODEL, W_ATT)), _resident((D_MODEL, W_ATT)),
                  _resident((D_MODEL, 2 * LANES)), _resident((1, 2 * LANES))],
        out_specs=[_row_spec(tm, W_ATT)] * (n_bf + 2) + [_row_spec(tm, N_HEADS_A), _row_spec(tm, LANES)],
        out_shape=[sd((rows, W_ATT), BF16)] * n_bf + [sd((rows, W_ATT), F32)] * 2
                  + [sd((rows, N_HEADS_A), F32), sd((rows, LANES), F32)],
        scratch_shapes=[pltpu.VMEM((8, LANES), F32)] if folded else [],
        compiler_params=_params(1),
        name="inproj_qkv_folded" if folded else "inproj_qkv",
    )(x, mod, mod, g, wq, wk, wv, ws, bs)


CONV_TAIL = CONV_W - 1
CONV_FIRST = CONV_PAD - CONV_TAIL
CONV_COL_CHUNK = 512


def _causal_conv_silu(cbuf_sc, x, cw, cb, cols, t):
    cbuf_sc[CONV_PAD:CONV_PAD + t, cols] = x
    conv = cb + cw[0:1, :] * cbuf_sc[CONV_FIRST:CONV_FIRST + t, cols]
    for i in range(1, CONV_W):
        conv = conv + cw[i:i + 1, :] * cbuf_sc[CONV_FIRST + i:CONV_FIRST + i + t, cols]
    tail = cbuf_sc[t + CONV_FIRST:t + CONV_PAD, cols]
    cbuf_sc[CONV_FIRST:CONV_PAD, cols] = tail
    return _silu(conv), tail


def _inproj_gates_kernel(x_ref, sh_ref, sc_ref, g_ref, wz_ref, wga_ref, wgs_ref, z_ref, ga_ref, gs_ref):
    u = _modulated_norm(x_ref[...], g_ref[...], sh_ref[...], sc_ref[...]).astype(BF16)
    z_ref[...] = jnp.dot(u, wz_ref[...], preferred_element_type=F32).astype(BF16)
    ga_ref[...] = jnp.dot(u, wga_ref[...], preferred_element_type=F32).astype(BF16)
    gs_ref[...] = jnp.dot(u, wgs_ref[...], preferred_element_type=F32).astype(BF16)


def _inproj_gates(x, mod, g, wz, wga, wgs, tm):
    rows = x.shape[0]
    sd = jax.ShapeDtypeStruct
    return pl.pallas_call(
        _inproj_gates_kernel,
        grid=(rows // tm,),
        in_specs=[_row_spec(tm, D_MODEL), _mod_spec(mod, tm, 3), _mod_spec(mod, tm, 4), _resident((1, D_MODEL)),
                  _resident((D_MODEL, D_INNER)), _resident((D_MODEL, D_MODEL)), _resident((D_MODEL, D_MODEL))],
        out_specs=[_row_spec(tm, D_INNER), _row_spec(tm, D_MODEL), _row_spec(tm, D_MODEL)],
        out_shape=[sd((rows, D_INNER), BF16), sd((rows, D_MODEL), BF16), sd((rows, D_MODEL), BF16)],
        compiler_params=_params(1),
        name="inproj_gates",
    )(x, mod, mod, g, wz, wga, wgs)


def _inproj_xbc_kernel(x_ref, sh_ref, sc_ref, g_ref, wx_ref, *refs, fused_conv):
    u = _modulated_norm(x_ref[...], g_ref[...], sh_ref[...], sc_ref[...]).astype(BF16)
    if not fused_conv:
        (xbc_ref,) = refs
        xbc_ref[...] = jnp.dot(u, wx_ref[...], preferred_element_type=F32)
        return
    cst_ref, cw_ref, cb_ref, xs_ref, bc_ref, cout_ref, cbuf_sc = refs
    tm = x_ref.shape[0]

    @pl.when(pl.program_id(0) == 0)
    def _():
        cbuf_sc[CONV_FIRST:CONV_PAD, :] = cst_ref[...]

    for c0 in range(0, CONV_DIM, CONV_COL_CHUNK):
        cols = slice(c0, c0 + CONV_COL_CHUNK)
        xbc = jnp.dot(u, wx_ref[:, cols], preferred_element_type=F32)
        xc, tail = _causal_conv_silu(cbuf_sc, xbc, cw_ref[:, cols], cb_ref[:, cols], cols, tm)
        if c0 < D_INNER:
            xs_ref[:, cols] = xc.astype(BF16)
        else:
            bc_ref[:, c0 - D_INNER:c0 - D_INNER + CONV_COL_CHUNK] = xc.astype(BF16)
        cout_ref[:, cols] = tail


def _inproj_xbc(x, mod, g, wx, tm, conv=None):
    rows = x.shape[0]
    sd = jax.ShapeDtypeStruct
    fused = conv is not None
    in_specs = [_row_spec(tm, D_MODEL), _mod_spec(mod, tm, 3), _mod_spec(mod, tm, 4), _resident((1, D_MODEL)),
                _resident((D_MODEL, CONV_DIM))]
    if fused:
        in_specs += [_resident((CONV_TAIL, CONV_DIM)), _resident((CONV_W, CONV_DIM)), _resident((1, CONV_DIM))]
        out_specs = [_row_spec(tm, D_INNER), _row_spec(tm, CONV_DIM - D_INNER),
                     pl.BlockSpec((CONV_TAIL, CONV_DIM), lambda i: (0, 0))]
        out_shape = [sd((rows, D_INNER), BF16), sd((rows, CONV_DIM - D_INNER), BF16),
                     sd((CONV_TAIL, CONV_DIM), F32)]
        scratch = [pltpu.VMEM((CONV_PAD + tm, CONV_DIM), F32)]
    else:
        out_specs = [_row_spec(tm, CONV_DIM)]
        out_shape = [sd((rows, CONV_DIM), F32)]
        scratch = []
    return pl.pallas_call(
        functools.partial(_inproj_xbc_kernel, fused_conv=fused),
        grid=(rows // tm,),
        in_specs=in_specs,
        out_specs=out_specs,
        out_shape=out_shape,
        scratch_shapes=scratch,
        compiler_params=_params(1),
        name="inproj_xbc_conv" if fused else "inproj_xbc",
    )(x, mod, mod, g, wx, *(conv or ()))


def _merge_kernel(x_ref, o_ref, y_ref, z_ref, ga_ref, gs_ref, g2_ref, gssd_ref, wa_ref, ws_ref, wo_ref,
                  out_ref):
    yg = y_ref[...].astype(F32) * _silu(z_ref[...].astype(F32))
    ms = jnp.mean(yg * yg, axis=-1, keepdims=True)
    yn = (yg * lax.rsqrt(ms + EPS) * gssd_ref[...]).astype(BF16)
    o_s = jnp.dot(yn, ws_ref[...], preferred_element_type=F32)
    o_a = jnp.dot(o_ref[...], wa_ref[...], preferred_element_type=F32)
    merged = (jax.nn.sigmoid(ga_ref[...].astype(F32)) * o_a
              + jax.nn.sigmoid(gs_ref[...].astype(F32)) * o_s)
    m = jnp.dot(merged.astype(BF16), wo_ref[...], preferred_element_type=F32)
    out_ref[...] = x_ref[...] + (1.0 + g2_ref[...]) * m


def _merge(x, o, y, z, ga, gs, mod, g_ssd, wa, ws, wo, tm):
    rows = x.shape[0]
    return pl.pallas_call(
        _merge_kernel,
        grid=(rows // tm,),
        in_specs=[_row_spec(tm, D_MODEL), _row_spec(tm, W_ATT), _row_spec(tm, D_INNER),
                  _row_spec(tm, D_INNER), _row_spec(tm, D_MODEL), _row_spec(tm, D_MODEL),
                  _mod_spec(mod, tm, 5), _resident((1, D_INNER)),
                  _resident((W_ATT, D_MODEL)), _resident((D_INNER, D_MODEL)), _resident((D_MODEL, D_MODEL))],
        out_specs=_row_spec(tm, D_MODEL),
        out_shape=jax.ShapeDtypeStruct((rows, D_MODEL), F32),
        compiler_params=_params(1),
        name="merge",
    )(x, o, y, z, ga, gs, mod, g_ssd, wa, ws, wo)


def _fcum_sample_kernel(past_ref, new_ref, fp_ref, fn_ref):
    p = past_ref.shape[1]
    upper_p = (_iota((p, p), 0) <= _iota((p, p), 1)).astype(BF16)
    upper_n = (_iota((LANES, LANES), 0) <= _iota((LANES, LANES), 1)).astype(BF16)
    past = past_ref[...]
    fp_ref[...] = _dot01_right(past, upper_p)
    total = _dot01_right(past, jnp.ones((p, LANES), BF16))
    fn_ref[...] = total + _dot01_right(new_ref[...], upper_n)


def _fcum_sample(past, new):
    rows, p = past.shape
    return pl.pallas_call(
        _fcum_sample_kernel,
        out_shape=[jax.ShapeDtypeStruct((rows, p), F32), jax.ShapeDtypeStruct((rows, LANES), F32)],
        compiler_params=pltpu.CompilerParams(vmem_limit_bytes=VMEM_LIMIT_BYTES),
        name="fcum_sample",
    )(past, new)


def _pair_normalize(acc_even, acc_odd):
    lane = _iota((1, LANES), 1)
    return jnp.where(lane < HEAD_DIM_A,
                     acc_even / acc_even[:, HEAD_DIM_A:HEAD_DIM_A + 1], acc_odd / acc_odd[:, 0:1])


def _attn_prompt_kernel(qe_ref, qo_ref, ke_ref, ko_ref, ve_ref, vo_ref, o_ref, m_sc, acc_sc):
    qi = pl.program_id(1)
    tq = qe_ref.shape[0]
    tk = ATT_KEY_BLOCK
    blocks_per_q = tq // tk
    q_refs, k_refs, v_refs = (qe_ref, qo_ref), (ke_ref, ko_ref), (ve_ref, vo_ref)
    m_sc[...] = jnp.full(m_sc.shape, -jnp.inf, F32)
    acc_sc[...] = jnp.zeros(acc_sc.shape, F32)

    def unit(r0, nrows, j, key_start):
        off = pl.multiple_of(j * tk, tk)
        rows = slice(r0, r0 + nrows)
        scores = []
        for hh in range(2):
            s = lax.dot_general(q_refs[hh][rows, :], k_refs[hh][pl.ds(off, tk), :], _NT,
                                preferred_element_type=F32)
            if key_start is not None:
                keep = (key_start + _iota((nrows, tk), 1)) <= (r0 + _iota((nrows, tk), 0))
                s = jnp.where(keep, s, -jnp.inf)
            scores.append(s)
        probs = []
        for hh in range(2):
            m_prev = m_sc[hh, rows, :]
            m_new = jnp.maximum(m_prev, jnp.max(scores[hh], axis=1, keepdims=True))
            p = jnp.exp(scores[hh] - jnp.concatenate([m_new] * (tk // LANES), axis=1))
            probs.append((p.astype(BF16), jnp.exp(m_prev - m_new)))
            m_sc[hh, rows, :] = m_new
        for hh in range(2):
            p, alpha = probs[hh]
            pv = jnp.dot(p, v_refs[hh][pl.ds(off, tk), :], preferred_element_type=F32)
            acc_sc[hh, rows, :] = alpha * acc_sc[hh, rows, :] + pv

    def run_blocks(first, count):
        for d in range(count):
            unit(0, tq, first + d, None)

    def body(trip, carry):
        run_blocks(trip * ATT_BLOCKS_PER_TRIP, ATT_BLOCKS_PER_TRIP)
        return carry

    first_diag = qi * blocks_per_q
    full_trips = first_diag // ATT_BLOCKS_PER_TRIP
    lax.fori_loop(0, full_trips, body, 0)
    left = first_diag - full_trips * ATT_BLOCKS_PER_TRIP
    count = ATT_BLOCKS_PER_TRIP // 2
    done = full_trips * ATT_BLOCKS_PER_TRIP
    while count >= blocks_per_q:
        has = (left // count) % 2 == 1

        @pl.when(has)
        def _(done=done, count=count):
            run_blocks(done, count)

        done = done + jnp.where(has, count, 0)
        count //= 2
    for d in range(blocks_per_q):
        unit(d * tk, tq - d * tk, first_diag + d, d * tk)

    o_ref[...] = _pair_normalize(acc_sc[0], acc_sc[1]).astype(o_ref.dtype)


def _attn_prompt(qe, qo, ke, ko, ve, vo):
    s = qe.shape[0]
    tq = ATT_QUERY_BLOCK
    n_pairs = W_ATT // LANES
    q_spec = pl.BlockSpec((tq, LANES), lambda p, i: (i, p))
    kv_spec = pl.BlockSpec((s, LANES), lambda p, i: (0, p))
    return pl.pallas_call(
        _attn_prompt_kernel,
        grid=(n_pairs, s // tq),
        in_specs=[q_spec, q_spec, kv_spec, kv_spec, kv_spec, kv_spec],
        out_specs=q_spec,
        out_shape=jax.ShapeDtypeStruct((s, W_ATT), BF16),
        scratch_shapes=[pltpu.VMEM((2, tq, LANES), F32), pltpu.VMEM((2, tq, LANES), F32)],
        compiler_params=_params(2),
        name="attn_prompt",
    )(qe, qo, ke, ko, ve, vo)


def _attn_sample_kernel(q_ref, kn_ref, vn_ref, kc_ref, vc_ref, fp_ref, fn_ref, o_ref):
    n = q_ref.shape[0]
    past = fp_ref.shape[1]
    keep = _iota((n, n), 1) <= _iota((n, n), 0)
    for h in range(N_HEADS_A):
        cols = slice(h * HEAD_DIM_A, (h + 1) * HEAD_DIM_A)
        q = q_ref[:, cols]
        k_past = kc_ref[pl.ds(h, past, stride=N_HEADS_A), :].astype(BF16)
        v_past = vc_ref[pl.ds(h, past, stride=N_HEADS_A), :].astype(BF16)
        s_past = lax.dot_general(q, k_past, _NT, preferred_element_type=F32) - fp_ref[h:h + 1, :]
        s_new = lax.dot_general(q, kn_ref[:, cols], _NT, preferred_element_type=F32) - fn_ref[h:h + 1, :n]
        s_new = jnp.where(keep, s_new, -jnp.inf)
        m = jnp.maximum(jnp.max(s_past, axis=1, keepdims=True), jnp.max(s_new, axis=1, keepdims=True))
        p_past = jnp.exp(s_past - m)
        p_new = jnp.exp(s_new - m)
        denom = jnp.sum(p_past, axis=1, keepdims=True) + jnp.sum(p_new, axis=1, keepdims=True)
        acc = (jnp.dot(p_past.astype(BF16), v_past, preferred_element_type=F32)
               + jnp.dot(p_new.astype(BF16), vn_ref[:, cols], preferred_element_type=F32))
        o_ref[:, cols] = (acc / denom).astype(o_ref.dtype)


def _attn_sample(q, kn, vn, cache_k, cache_v, f_past, f_new, n_batch, n_new):
    past = f_past.shape[2]
    tok = pl.BlockSpec((n_new, W_ATT), lambda b: (b, 0))
    cache = pl.BlockSpec((None, past * N_HEADS_A, HEAD_DIM_A), lambda b: (b, 0, 0))
    return pl.pallas_call(
        _attn_sample_kernel,
        grid=(n_batch,),
        in_specs=[tok, tok, tok, cache, cache,
                  pl.BlockSpec((None, N_HEADS_A, past), lambda b: (b, 0, 0)),
                  pl.BlockSpec((None, N_HEADS_A, LANES), lambda b: (b, 0, 0))],
        out_specs=tok,
        out_shape=jax.ShapeDtypeStruct((n_batch * n_new, W_ATT), BF16),
        compiler_params=_params(1),
        name="attn_sample",
    )(q, kn, vn, cache_k, cache_v, f_past, f_new)


def _ssd_chunk(xs, b_all, c_all, dt, alog_ref, dsk_ref, y_ref, ht_sc):
    t = xs.shape[0]
    hp_group = D_INNER // SSD_GROUPS
    heads_per_group = SSD_HEADS // SSD_GROUPS
    lane = _iota((1, LANES), 1)

    a_row = -jnp.exp(alog_ref[...])
    lower = (_iota((t, t), 1) <= _iota((t, t), 0)).astype(BF16)
    acs = _dot01_left(lower, dt * a_row)
    acs_end = acs[t - 1:t, :]
    compact = jnp.concatenate([jnp.exp(acs_end - acs) * dt, dt, jnp.exp(acs)], axis=0)
    expand = ((_iota((LANES, D_INNER), 1) // SSD_HEADDIM) == _iota((LANES, D_INNER), 0)).astype(BF16)
    wide = _dot01_right(compact, expand, pieces=2)
    xs_state = (xs * wide[0:t]).astype(BF16)
    xs_dt = (xs * wide[t:2 * t]).astype(BF16)
    e_acs = wide[2 * t:3 * t]
    chunk_decay = e_acs[t - 1:t, :]

    eye = (_iota((LANES, LANES), 0) == _iota((LANES, LANES), 1)).astype(BF16)
    acs_t = _transpose01(eye, acs)
    causal = _iota((t, t), 1) <= _iota((t, t), 0)

    for g in range(SSD_GROUPS):
        b_g = b_all[:, g * SSD_STATE:(g + 1) * SSD_STATE]
        c_g = c_all[:, g * SSD_STATE:(g + 1) * SSD_STATE]
        cols = slice(g * hp_group, (g + 1) * hp_group)
        ht_g = ht_sc[:, cols]
        y_off = jnp.dot(c_g, ht_g.astype(BF16), preferred_element_type=F32) * e_acs[:, cols]
        states = lax.dot_general(b_g, xs_state[:, cols], _TN, preferred_element_type=F32)
        ht_sc[:, cols] = ht_g * chunk_decay[:, cols] + states
        cb = lax.dot_general(c_g, b_g, _NT, preferred_element_type=F32)
        for pair in range(heads_per_group // 2):
            h0 = g * heads_per_group + 2 * pair
            lo = h0 * SSD_HEADDIM
            x_pair = xs_dt[:, lo:lo + LANES]
            y_heads = []
            for h in (h0, h0 + 1):
                seg = acs[:, h:h + 1] - acs_t[h:h + 1, :]
                w = (cb * jnp.exp(jnp.where(causal, seg, -jnp.inf))).astype(BF16)
                y_heads.append(jnp.dot(w, x_pair, preferred_element_type=F32))
            y_pair = (y_off[:, lo - g * hp_group:lo - g * hp_group + LANES]
                      + jnp.where(lane < SSD_HEADDIM, y_heads[0], y_heads[1])
                      + dsk_ref[:, lo:lo + LANES] * xs[:, lo:lo + LANES])
            y_ref[:, lo:lo + LANES] = y_pair.astype(y_ref.dtype)


def _load_state_transposed(h0_ref, ht_sc):
    for b in range(D_INNER // LANES):
        ht_sc[:, b * LANES:(b + 1) * LANES] = h0_ref[b * LANES:(b + 1) * LANES, :].T


def _store_state(ht_sc, hout_ref):
    for b in range(D_INNER // LANES):
        hout_ref[b * LANES:(b + 1) * LANES, :] = ht_sc[:, b * LANES:(b + 1) * LANES].T


def _ssd_conv_kernel(xbc_ref, dt_ref, cst_ref, h0_ref, cw_ref, cb_ref, alog_ref, dsk_ref,
                     y_ref, hout_ref, cout_ref, ht_sc, cbuf_sc):
    c = pl.program_id(1)
    t = xbc_ref.shape[0]

    @pl.when(c == 0)
    def _():
        cbuf_sc[CONV_FIRST:CONV_PAD, :] = cst_ref[...]
        _load_state_transposed(h0_ref, ht_sc)

    xc, tail = _causal_conv_silu(cbuf_sc, xbc_ref[...], cw_ref[...], cb_ref[...], slice(None), t)
    _ssd_chunk(xc[:, :D_INNER], xc[:, D_INNER:D_INNER + SSD_GROUPS * SSD_STATE].astype(BF16),
               xc[:, D_INNER + SSD_GROUPS * SSD_STATE:].astype(BF16), dt_ref[...], alog_ref, dsk_ref, y_ref, ht_sc)

    @pl.when(c == pl.num_programs(1) - 1)
    def _():
        cout_ref[...] = tail
        _store_state(ht_sc, hout_ref)


def _ssd_kernel(xs_ref, bc_ref, dt_ref, h0_ref, alog_ref, dsk_ref, y_ref, hout_ref, ht_sc):
    c = pl.program_id(1)

    @pl.when(c == 0)
    def _():
        _load_state_transposed(h0_ref, ht_sc)

    gn = SSD_GROUPS * SSD_STATE
    _ssd_chunk(xs_ref[...].astype(F32), bc_ref[:, :gn], bc_ref[:, gn:], dt_ref[...], alog_ref, dsk_ref,
               y_ref, ht_sc)

    @pl.when(c == pl.num_programs(1) - 1)
    def _():
        _store_state(ht_sc, hout_ref)


def _ssd(x_in, dt, ssm_state, a_log_row, d_skip_row, n_batch, seq, chunk, conv=None):
    assert seq % chunk == 0 and chunk >= CONV_TAIL
    nc = seq // chunk
    rows = lambda b, c: (b * nc + c, 0)
    per_batch3 = lambda b, c: (b, 0, 0)
    const2 = lambda b, c: (0, 0)
    sd = jax.ShapeDtypeStruct
    state_spec = pl.BlockSpec((None, D_INNER, SSD_STATE), per_batch3)
    tail_specs = [pl.BlockSpec((1, LANES), const2), pl.BlockSpec((1, D_INNER), const2)]
    y_spec = pl.BlockSpec((chunk, D_INNER), rows)
    y_shape = sd((n_batch * seq, D_INNER), BF16)
    state_shape = sd((n_batch, D_INNER, SSD_STATE), F32)
    state_scratch = pltpu.VMEM((SSD_STATE, D_INNER), F32)
    if conv is None:
        xs, bc = x_in
        y, h_new = pl.pallas_call(
            _ssd_kernel,
            grid=(n_batch, nc),
            in_specs=[pl.BlockSpec((chunk, D_INNER), rows), pl.BlockSpec((chunk, CONV_DIM - D_INNER), rows),
                      pl.BlockSpec((chunk, LANES), rows), state_spec] + tail_specs,
            out_specs=[y_spec, state_spec],
            out_shape=[y_shape, state_shape],
            scratch_shapes=[state_scratch],
            compiler_params=_params(2),
            name="ssd_scan",
        )(xs, bc, dt, ssm_state, a_log_row, d_skip_row)
        return y, h_new, None
    conv_state, conv_w, conv_b = conv
    conv_spec = pl.BlockSpec((None, CONV_TAIL, CONV_DIM), per_batch3)
    return pl.pallas_call(
        _ssd_conv_kernel,
        grid=(n_batch, nc),
        in_specs=[pl.BlockSpec((chunk, CONV_DIM), rows), pl.BlockSpec((chunk, LANES), rows),
                  conv_spec, state_spec,
                  pl.BlockSpec((CONV_W, CONV_DIM), const2), pl.BlockSpec((1, CONV_DIM), const2)] + tail_specs,
        out_specs=[y_spec, state_spec, conv_spec],
        out_shape=[y_shape, state_shape, sd((n_batch, CONV_TAIL, CONV_DIM), F32)],
        scratch_shapes=[state_scratch, pltpu.VMEM((CONV_PAD + chunk, CONV_DIM), F32)],
        compiler_params=_params(2),
        name="ssd_conv_scan",
    )(x_in, dt, conv_state, ssm_state, conv_w, conv_b, a_log_row, d_skip_row)


def _pad_lanes(row, width=LANES):
    return jnp.pad(row, ((0, 0), (0, width - row.shape[1])))


def _layer_weights(w_in, b_f, dt_bias, a_log, d_skip, prm):
    o = 0
    pieces = {}
    for name, width in (("q", W_ATT), ("k", W_ATT), ("v", W_ATT), ("f", N_HEADS_A), ("z", D_INNER),
                        ("xbc", CONV_DIM), ("dt", SSD_HEADS), ("ga", D_MODEL), ("gs", D_MODEL)):
        pieces[name] = w_in[:, o:o + width]
        o += width
    w = {k: pieces[k].astype(BF16) for k in ("q", "k", "v", "z", "xbc", "ga", "gs")}
    w["small"] = jnp.concatenate([_pad_lanes(pieces["f"]), _pad_lanes(pieces["dt"])], axis=1).astype(BF16)
    w["small_bias"] = jnp.concatenate([_pad_lanes(b_f[None, :]), _pad_lanes(dt_bias[None, :])], axis=1)
    w["a_log_row"] = _pad_lanes(a_log[None, :])
    w["d_skip_row"] = jnp.repeat(d_skip, SSD_HEADDIM)[None, :]
    for k, v in prm.items():
        w[k] = v.astype(BF16) if v.ndim == 2 and v.shape[0] >= D_MODEL else v
    return w


def _mixer_and_ffn(x, mod, w, tm, g_final, attend, single_sequence, conv_state, ssm_state, n_batch, seq, chunk):
    x, _ = _ffn(x, mod, 0, w["g_ffn1"], w["w1_ffn1"], w["w3_ffn1"], w["w2_ffn1"], g_final, tm, False)
    *att_ops, k32, v32, logf, dt = _inproj_a(x, mod, w["g_mix"], w["q"], w["k"], w["v"],
                                             w["small"], w["small_bias"], tm, single_sequence)
    o = attend(att_ops, logf)
    z, ga, gs = _inproj_gates(x, mod, w["g_mix"], w["z"], w["ga"], w["gs"], tm)
    if single_sequence:
        xs, bc, conv_new = _inproj_xbc(x, mod, w["g_mix"], w["xbc"], tm, (conv_state[0], w["conv_w"], w["conv_b"]))
        y, ssm_new, _ = _ssd((xs, bc), dt, ssm_state, w["a_log_row"], w["d_skip_row"], n_batch, seq, chunk)
        conv_new = conv_new[None]
    else:
        (xbc,) = _inproj_xbc(x, mod, w["g_mix"], w["xbc"], tm)
        y, ssm_new, conv_new = _ssd(xbc, dt, ssm_state, w["a_log_row"], w["d_skip_row"], n_batch, seq, chunk,
                                    (conv_state, w["conv_w"], w["conv_b"]))
    x = _merge(x, o, y, z, ga, gs, mod, w["g_ssd"], w["w_a"], w["w_s"], w["w_out"], tm)
    x, y_final = _ffn(x, mod, 6, w["g_ffn2"], w["w1_ffn2"], w["w3_ffn2"], w["w2_ffn2"], g_final, tm, True)
    return x, y_final, (k32, v32, logf, ssm_new, conv_new)


def kernel(x_prompt, x_sample, c_prompt, c_sample, cache_k, cache_v, cache_logf, state_ssm, state_conv,
           w_ada, b_ada, g_ffn1, w1_ffn1, w3_ffn1, w2_ffn1, g_mix, w_in, b_f, conv_w, conv_b,
           dt_bias, a_log, d_skip, g_ssd, w_a, w_s, w_out, g_ffn2, w1_ffn2, w3_ffn2, w2_ffn2, g_final):
    depth = w_ada.shape[0]
    bp, seq_p, _ = x_prompt.shape
    bs, seq_s, _ = x_sample.shape
    past = cache_k.shape[2]
    assert bp == 1

    xp = x_prompt.reshape(bp * seq_p, D_MODEL)
    xs = x_sample.reshape(bs * seq_s, D_MODEL)
    c_rows = bs + bp
    c_all = jnp.pad(jnp.concatenate([c_sample, c_prompt], axis=0), ((0, -c_rows % 8), (0, 0)))
    gf = g_final[None, :]

    outs_p = [[] for _ in range(5)]
    outs_s = [[] for _ in range(5)]
    yp = ys = None
    for l in range(depth):
        prm = {"g_ffn1": g_ffn1[l][None, :], "w1_ffn1": w1_ffn1[l], "w3_ffn1": w3_ffn1[l], "w2_ffn1": w2_ffn1[l],
               "g_mix": g_mix[l][None, :], "conv_w": conv_w[l], "conv_b": conv_b[l][None, :],
               "g_ssd": g_ssd[l][None, :], "w_a": w_a[l], "w_s": w_s[l], "w_out": w_out[l],
               "g_ffn2": g_ffn2[l][None, :], "w1_ffn2": w1_ffn2[l], "w3_ffn2": w3_ffn2[l], "w2_ffn2": w2_ffn2[l]}
        w = _layer_weights(w_in[l], b_f[l], dt_bias[l], a_log[l], d_skip[l], prm)

        mod = _ada(c_all, w_ada[l], b_ada[l])
        mod_s = jnp.repeat(mod[:bs], seq_s, axis=0)
        mod_p = mod[bs:bs + 1]

        def attend_prompt(att_ops, logf):
            return _attn_prompt(*att_ops)

        def attend_sample(att_ops, logf):
            q, kb, vb = att_ops
            past_rows = cache_logf[l].transpose(0, 2, 1).reshape(bs * N_HEADS_A, past)
            new_rows = logf.reshape(bs, seq_s, N_HEADS_A).transpose(0, 2, 1).reshape(bs * N_HEADS_A, seq_s)
            f_past, f_new = _fcum_sample(past_rows, _pad_lanes(new_rows))
            return _attn_sample(q, kb, vb,
                                cache_k[l].reshape(bs, past * N_HEADS_A, HEAD_DIM_A),
                                cache_v[l].reshape(bs, past * N_HEADS_A, HEAD_DIM_A),
                                f_past.reshape(bs, N_HEADS_A, past), f_new.reshape(bs, N_HEADS_A, LANES),
                                bs, seq_s)

        zero_conv = jnp.zeros((bp, CONV_W - 1, CONV_DIM), F32)
        zero_ssm = jnp.zeros((bp, D_INNER, SSD_STATE), F32)
        xp, yp, new_p = _mixer_and_ffn(xp, mod_p, w, ROW_TILE, gf, attend_prompt, True, zero_conv, zero_ssm,
                                       bp, seq_p, SSD_CHUNK_PROMPT)
        xs, ys, new_s = _mixer_and_ffn(xs, mod_s, w, bs * seq_s, gf, attend_sample, False, state_conv[l],
                                       state_ssm[l].reshape(bs, D_INNER, SSD_STATE), bs, seq_s, seq_s)
        for dst, new, nb, sq in ((outs_p, new_p, bp, seq_p), (outs_s, new_s, bs, seq_s)):
            k32, v32, logf, ssm_new, conv_new = new
            dst[0].append(k32.reshape(nb, sq, N_HEADS_A, HEAD_DIM_A))
            dst[1].append(v32.reshape(nb, sq, N_HEADS_A, HEAD_DIM_A))
            dst[2].append(logf.reshape(nb, sq, N_HEADS_A))
            dst[3].append(ssm_new.reshape(nb, SSD_HEADS, SSD_HEADDIM, SSD_STATE))
            dst[4].append(conv_new)

    y_prompt = yp.reshape(bp, seq_p, D_MODEL)
    y_sample = ys.reshape(bs, seq_s, D_MODEL)
    return (y_prompt, y_sample, *[jnp.stack(o) for o in outs_p], *[jnp.stack(o) for o in outs_s])
```

```python
import functools

import jax
import jax.numpy as jnp
from jax import lax
from jax.experimental import pallas as pl
from jax.experimental.pallas import tpu as pltpu

F32 = jnp.float32
BF16 = jnp.bfloat16

EPS = 1e-6
D_MODEL = 1024
N_HEADS_A = 16
HEAD_DIM_A = 64
W_ATT = N_HEADS_A * HEAD_DIM_A
D_INNER = 2 * D_MODEL
SSD_HEADDIM = 64
SSD_HEADS = D_INNER // SSD_HEADDIM
SSD_STATE = 128
SSD_GROUPS = 4
CONV_W = 4
CONV_DIM = D_INNER + 2 * SSD_GROUPS * SSD_STATE
D_FF = 2816
N_MOD = 9

LANES = 128
VMEM_LIMIT_BYTES = 56 * 1024 * 1024

ROW_TILE = 512
ROW_SUBTILE = 256
FF_CHUNK = 1408
ATT_QUERY_BLOCK = 1024
ATT_KEY_BLOCK = 512
ATT_BLOCKS_PER_TRIP = 4
SSD_CHUNK_PROMPT = 256
CONV_PAD = 8

_NT = (((1,), (1,)), ((), ()))
_TN = (((0,), (0,)), ((), ()))


def _resident(shape):
    nd = len(shape)
    return pl.BlockSpec(shape, lambda *_: (0,) * nd, pipeline_mode=pl.Buffered(1))


def _params(n_axes):
    return pltpu.CompilerParams(dimension_semantics=("arbitrary",) * n_axes,
                                vmem_limit_bytes=VMEM_LIMIT_BYTES)


def _silu(x):
    return x * jax.nn.sigmoid(x)


def _softplus(x):
    return jnp.maximum(x, 0.0) + jnp.log1p(jnp.exp(-jnp.abs(x)))


def _split3(x, pieces=3):
    out = []
    for _ in range(pieces - 1):
        hi = x.astype(BF16)
        out.append(hi)
        x = x - hi.astype(F32)
    out.append(x.astype(BF16))
    return out


def _dot01_right(x, u01, pieces=3):
    acc = None
    for piece in _split3(x, pieces):
        t = jnp.dot(piece, u01, preferred_element_type=F32)
        acc = t if acc is None else acc + t
    return acc


def _dot01_left(l01, x):
    acc = None
    for piece in _split3(x):
        t = jnp.dot(l01, piece, preferred_element_type=F32)
        acc = t if acc is None else acc + t
    return acc


def _transpose01(eye, x):
    acc = None
    for piece in _split3(x):
        t = lax.dot_general(eye, piece, _NT, preferred_element_type=F32)
        acc = t if acc is None else acc + t
    return acc


def _iota(shape, dim):
    return lax.broadcasted_iota(jnp.int32, shape, dim)


def _modulated_norm(x, g, shift, scale):
    ms = jnp.mean(x * x, axis=-1, keepdims=True)
    h = x * lax.rsqrt(ms + EPS) * g
    return h * (1.0 + scale) + shift


def _ada_kernel(c_ref, w_ref, b_ref, o_ref):
    c = _silu(c_ref[...]).astype(BF16)
    o_ref[...] = jnp.dot(c, w_ref[...].astype(BF16), preferred_element_type=F32) + b_ref[...]


def _ada(c_all, w_ada, b_ada):
    rows = c_all.shape[0]
    n = w_ada.shape[1]
    tn = 1536
    return pl.pallas_call(
        _ada_kernel,
        grid=(n // tn,),
        in_specs=[pl.BlockSpec((rows, D_MODEL), lambda j: (0, 0)),
                  pl.BlockSpec((D_MODEL, tn), lambda j: (0, j)),
                  pl.BlockSpec((1, tn), lambda j: (0, j))],
        out_specs=pl.BlockSpec((rows, tn), lambda j: (0, j)),
        out_shape=jax.ShapeDtypeStruct((rows, n), F32),
        compiler_params=_params(1),
        name="ada_mod",
    )(c_all, w_ada, b_ada.reshape(1, n))


def _mod_spec(mod, tm, k):
    if mod.shape[0] == 1:
        return pl.BlockSpec((1, D_MODEL), lambda i: (0, k))
    return pl.BlockSpec((tm, D_MODEL), lambda i: (i, k))


def _row_spec(tm, width):
    return pl.BlockSpec((tm, width), lambda i: (i, 0))


def _row_subtiles(tm):
    sub = min(tm, ROW_SUBTILE)
    return [slice(r0, r0 + sub) for r0 in range(0, tm, sub)]


def _mod_rows(mod_ref, rows):
    return mod_ref[...] if mod_ref.shape[0] == 1 else mod_ref[rows, :]


def _ffn_kernel(x_ref, sh_ref, sc_ref, ga_ref, g_ref, w1_ref, w3_ref, w2_ref, gf_ref, o_ref, *maybe_y_ref,
                final_norm):
    x = x_ref[...]
    h = _modulated_norm(x, g_ref[...], sh_ref[...], sc_ref[...]).astype(BF16)
    y = None
    for c0 in range(0, D_FF, FF_CHUNK):
        a = jnp.dot(h, w1_ref[:, c0:c0 + FF_CHUNK], preferred_element_type=F32)
        b = jnp.dot(h, w3_ref[:, c0:c0 + FF_CHUNK], preferred_element_type=F32)
        t = jnp.dot((_silu(a) * b).astype(BF16), w2_ref[c0:c0 + FF_CHUNK, :], preferred_element_type=F32)
        y = t if y is None else y + t
    out = x + 0.5 * (1.0 + ga_ref[...]) * y
    o_ref[...] = out
    if final_norm:
        (y_ref,) = maybe_y_ref
        ms = jnp.mean(out * out, axis=-1, keepdims=True)
        y_ref[...] = out * lax.rsqrt(ms + EPS) * gf_ref[...]


def _ffn(x, mod, k0, g, w1, w3, w2, g_final, tm, final_norm):
    rows = x.shape[0]
    n_out = 2 if final_norm else 1
    res = pl.pallas_call(
        functools.partial(_ffn_kernel, final_norm=final_norm),
        grid=(rows // tm,),
        in_specs=[_row_spec(tm, D_MODEL),
                  _mod_spec(mod, tm, k0), _mod_spec(mod, tm, k0 + 1), _mod_spec(mod, tm, k0 + 2),
                  _resident((1, D_MODEL)),
                  _resident((D_MODEL, D_FF)), _resident((D_MODEL, D_FF)), _resident((D_FF, D_MODEL)),
                  _resident((1, D_MODEL))],
        out_specs=[_row_spec(tm, D_MODEL)] * n_out,
        out_shape=[jax.ShapeDtypeStruct((rows, D_MODEL), F32)] * n_out,
        compiler_params=_params(1),
        name="ffn_final" if final_norm else "ffn",
    )(x, mod, mod, mod, g, w1, w3, w2, g_final)
    return (res[0], res[1]) if final_norm else (res[0], None)


N_FOLD = 3


def _fold_selectors():
    r = _iota((LANES, W_ATT), 0)
    j = _iota((LANES, W_ATT), 1)
    piece, head = r // N_HEADS_A, r % N_HEADS_A
    same_pair = (j // LANES) == (head // 2)
    valid = r < N_FOLD * N_HEADS_A
    lane = j % LANES
    even = valid & same_pair & (head % 2 == 0) & (lane == HEAD_DIM_A + piece)
    odd = valid & same_pair & (head % 2 == 1) & (lane == piece)
    return even.astype(BF16), odd.astype(BF16)


def _inproj_a_kernel(x_ref, sh_ref, sc_ref, g_ref, wq_ref, wk_ref, wv_ref, ws_ref, bs_ref, *refs, folded):
    tm = x_ref.shape[0]
    lane = _iota((1, LANES), 1)
    if folded:
        qe_ref, qo_ref, ke_ref, ko_ref, ve_ref, vo_ref, k32_ref, v32_ref, lf_ref, dt_ref, carry_sc = refs

        @pl.when(pl.program_id(0) == 0)
        def _():
            carry_sc[...] = jnp.zeros(carry_sc.shape, F32)

        sel_even, sel_odd = _fold_selectors()
        lip = _iota((1, W_ATT), 1) % LANES
        in_even, in_odd = lip < HEAD_DIM_A, lip >= HEAD_DIM_A
        ones_even = ((lip >= HEAD_DIM_A) & (lip < HEAD_DIM_A + N_FOLD)).astype(F32)
        ones_odd = (lip < N_FOLD).astype(F32)
    else:
        q_ref, kb_ref, vb_ref, k32_ref, v32_ref, lf_ref, dt_ref = refs

    for rows in _row_subtiles(tm):
        sub = rows.stop - rows.start
        u = _modulated_norm(x_ref[rows, :], g_ref[...], _mod_rows(sh_ref, rows), _mod_rows(sc_ref, rows)).astype(BF16)
        q = jnp.dot(u, wq_ref[...], preferred_element_type=F32) * (HEAD_DIM_A ** -0.5)
        k = jnp.dot(u, wk_ref[...], preferred_element_type=F32)
        v = jnp.dot(u, wv_ref[...], preferred_element_type=F32)
        sm = jnp.dot(u, ws_ref[...], preferred_element_type=F32) + bs_ref[...]
        logf = jnp.where(lane < N_HEADS_A, -_softplus(-sm[:, :LANES]), 0.0)
        dt = jnp.where(lane < SSD_HEADS, _softplus(sm[:, LANES:]), 0.0)
        if not folded:
            q_ref[rows, :] = q.astype(BF16)
            kb_ref[rows, :] = k.astype(BF16)
            vb_ref[rows, :] = v.astype(BF16)
        else:
            lower = (_iota((sub, sub), 1) <= _iota((sub, sub), 0)).astype(BF16)
            f_cum = _dot01_left(lower, logf) + carry_sc[0:1, :]
            carry_sc[...] = jnp.broadcast_to(f_cum[sub - 1:sub, :], carry_sc.shape)
            hi, mid, lo = _split3(-f_cum)
            packed = (hi.astype(F32) + pltpu.roll(mid.astype(F32), N_HEADS_A, 1)
                      + pltpu.roll(lo.astype(F32), 2 * N_HEADS_A, 1)).astype(BF16)
            fold_even = jnp.dot(packed, sel_even, preferred_element_type=F32)
            fold_odd = jnp.dot(packed, sel_odd, preferred_element_type=F32)
            qe_ref[rows, :] = jnp.where(in_even, q, ones_even).astype(BF16)
            qo_ref[rows, :] = jnp.where(in_odd, q, ones_odd).astype(BF16)
            ke_ref[rows, :] = jnp.where(in_even, k, fold_even).astype(BF16)
            ko_ref[rows, :] = jnp.where(in_odd, k, fold_odd).astype(BF16)
            ve_ref[rows, :] = jnp.where(in_even, v, (lip == HEAD_DIM_A).astype(F32)).astype(BF16)
            vo_ref[rows, :] = jnp.where(in_odd, v, (lip == 0).astype(F32)).astype(BF16)
        k32_ref[rows, :] = k
        v32_ref[rows, :] = v
        lf_ref[rows, :] = logf[:, :N_HEADS_A]
        dt_ref[rows, :] = dt


def _inproj_a(x, mod, g, wq, wk, wv, ws, bs, tm, folded):
    rows = x.shape[0]
    sd = jax.ShapeDtypeStruct
    n_bf = 6 if folded else 3
    return pl.pallas_call(
        functools.partial(_inproj_a_kernel, folded=folded),
        grid=(rows // tm,),
        in_specs=[_row_spec(tm, D_MODEL), _mod_spec(mod, tm, 3), _mod_spec(mod, tm, 4),
                  _resident((1, D_MODEL)),
                  _resident((D_MODEL, W_ATT)), _resident((D_MODEL, W_ATT)), _resident((D_MODEL, W_ATT)),
                  _resident((D_MODEL, 2 * LANES)), _resident((1, 2 * LANES))],
        out_specs=[_row_spec(tm, W_ATT)] * (n_bf + 2) + [_row_spec(tm, N_HEADS_A), _row_spec(tm, LANES)],
        out_shape=[sd((rows, W_ATT), BF16)] * n_bf + [sd((rows, W_ATT), F32)] * 2
                  + [sd((rows, N_HEADS_A), F32), sd((rows, LANES), F32)],
        scratch_shapes=[pltpu.VMEM((8, LANES), F32)] if folded else [],
        compiler_params=_params(1),
        name="inproj_qkv_folded" if folded else "inproj_qkv",
    )(x, mod, mod, g, wq, wk, wv, ws, bs)


CONV_TAIL = CONV_W - 1
CONV_FIRST = CONV_PAD - CONV_TAIL
CONV_COL_CHUNK = 512


def _causal_conv_silu(cbuf_sc, x, cw, cb, cols, t):
    cbuf_sc[CONV_PAD:CONV_PAD + t, cols] = x
    conv = cb + cw[0:1, :] * cbuf_sc[CONV_FIRST:CONV_FIRST + t, cols]
    for i in range(1, CONV_W):
        conv = conv + cw[i:i + 1, :] * cbuf_sc[CONV_FIRST + i:CONV_FIRST + i + t, cols]
    tail = cbuf_sc[t + CONV_FIRST:t + CONV_PAD, cols]
    cbuf_sc[CONV_FIRST:CONV_PAD, cols] = tail
    return _silu(conv), tail


def _inproj_gates_kernel(x_ref, sh_ref, sc_ref, g_ref, wz_ref, wga_ref, wgs_ref, z_ref, ga_ref, gs_ref):
    u = _modulated_norm(x_ref[...], g_ref[...], sh_ref[...], sc_ref[...]).astype(BF16)
    z_ref[...] = jnp.dot(u, wz_ref[...], preferred_element_type=F32).astype(BF16)
    ga_ref[...] = jnp.dot(u, wga_ref[...], preferred_element_type=F32).astype(BF16)
    gs_ref[...] = jnp.dot(u, wgs_ref[...], preferred_element_type=F32).astype(BF16)


def _inproj_gates(x, mod, g, wz, wga, wgs, tm):
    rows = x.shape[0]
    sd = jax.ShapeDtypeStruct
    return pl.pallas_call(
        _inproj_gates_kernel,
        grid=(rows // tm,),
        in_specs=[_row_spec(tm, D_MODEL), _mod_spec(mod, tm, 3), _mod_spec(mod, tm, 4), _resident((1, D_MODEL)),
                  _resident((D_MODEL, D_INNER)), _resident((D_MODEL, D_MODEL)), _resident((D_MODEL, D_MODEL))],
        out_specs=[_row_spec(tm, D_INNER), _row_spec(tm, D_MODEL), _row_spec(tm, D_MODEL)],
        out_shape=[sd((rows, D_INNER), BF16), sd((rows, D_MODEL), BF16), sd((rows, D_MODEL), BF16)],
        compiler_params=_params(1),
        name="inproj_gates",
    )(x, mod, mod, g, wz, wga, wgs)


def _inproj_xbc_kernel(x_ref, sh_ref, sc_ref, g_ref, wx_ref, *refs, fused_conv):
    u = _modulated_norm(x_ref[...], g_ref[...], sh_ref[...], sc_ref[...]).astype(BF16)
    if not fused_conv:
        (xbc_ref,) = refs
        xbc_ref[...] = jnp.dot(u, wx_ref[...], preferred_element_type=F32)
        return
    cst_ref, cw_ref, cb_ref, xs_ref, bc_ref, cout_ref, cbuf_sc = refs
    tm = x_ref.shape[0]

    @pl.when(pl.program_id(0) == 0)
    def _():
        cbuf_sc[CONV_FIRST:CONV_PAD, :] = cst_ref[...]

    for c0 in range(0, CONV_DIM, CONV_COL_CHUNK):
        cols = slice(c0, c0 + CONV_COL_CHUNK)
        xbc = jnp.dot(u, wx_ref[:, cols], preferred_element_type=F32)
        xc, tail = _causal_conv_silu(cbuf_sc, xbc, cw_ref[:, cols], cb_ref[:, cols], cols, tm)
        if c0 < D_INNER:
            xs_ref[:, cols] = xc.astype(BF16)
        else:
            bc_ref[:, c0 - D_INNER:c0 - D_INNER + CONV_COL_CHUNK] = xc.astype(BF16)
        cout_ref[:, cols] = tail


def _inproj_xbc(x, mod, g, wx, tm, conv=None):
    rows = x.shape[0]
    sd = jax.ShapeDtypeStruct
    fused = conv is not None
    in_specs = [_row_spec(tm, D_MODEL), _mod_spec(mod, tm, 3), _mod_spec(mod, tm, 4), _resident((1, D_MODEL)),
                _resident((D_MODEL, CONV_DIM))]
    if fused:
        in_specs += [_resident((CONV_TAIL, CONV_DIM)), _resident((CONV_W, CONV_DIM)), _resident((1, CONV_DIM))]
        out_specs = [_row_spec(tm, D_INNER), _row_spec(tm, CONV_DIM - D_INNER),
                     pl.BlockSpec((CONV_TAIL, CONV_DIM), lambda i: (0, 0))]
        out_shape = [sd((rows, D_INNER), BF16), sd((rows, CONV_DIM - D_INNER), BF16),
                     sd((CONV_TAIL, CONV_DIM), F32)]
        scratch = [pltpu.VMEM((CONV_PAD + tm, CONV_DIM), F32)]
    else:
        out_specs = [_row_spec(tm, CONV_DIM)]
        out_shape = [sd((rows, CONV_DIM), F32)]
        scratch = []
    return pl.pallas_call(
        functools.partial(_inproj_xbc_kernel, fused_conv=fused),
        grid=(rows // tm,),
        in_specs=in_specs,
        out_specs=out_specs,
        out_shape=out_shape,
        scratch_shapes=scratch,
        compiler_params=_params(1),
        name="inproj_xbc_conv" if fused else "inproj_xbc",
    )(x, mod, mod, g, wx, *(conv or ()))


def _merge_kernel(x_ref, o_ref, y_ref, z_ref, ga_ref, gs_ref, g2_ref, gssd_ref, wa_ref, ws_ref, wo_ref,
                  out_ref):
    for rows in _row_subtiles(x_ref.shape[0]):
        yg = y_ref[rows, :].astype(F32) * _silu(z_ref[rows, :].astype(F32))
        ms = jnp.mean(yg * yg, axis=-1, keepdims=True)
        yn = (yg * lax.rsqrt(ms + EPS) * gssd_ref[...]).astype(BF16)
        o_s = jnp.dot(yn, ws_ref[...], preferred_element_type=F32)
        o_a = jnp.dot(o_ref[rows, :], wa_ref[...], preferred_element_type=F32)
        merged = (jax.nn.sigmoid(ga_ref[rows, :].astype(F32)) * o_a
                  + jax.nn.sigmoid(gs_ref[rows, :].astype(F32)) * o_s)
        m = jnp.dot(merged.astype(BF16), wo_ref[...], preferred_element_type=F32)
        out_ref[rows, :] = x_ref[rows, :] + (1.0 + _mod_rows(g2_ref, rows)) * m


def _merge(x, o, y, z, ga, gs, mod, g_ssd, wa, ws, wo, tm):
    rows = x.shape[0]
    return pl.pallas_call(
        _merge_kernel,
        grid=(rows // tm,),
        in_specs=[_row_spec(tm, D_MODEL), _row_spec(tm, W_ATT), _row_spec(tm, D_INNER),
                  _row_spec(tm, D_INNER), _row_spec(tm, D_MODEL), _row_spec(tm, D_MODEL),
                  _mod_spec(mod, tm, 5), _resident((1, D_INNER)),
                  _resident((W_ATT, D_MODEL)), _resident((D_INNER, D_MODEL)), _resident((D_MODEL, D_MODEL))],
        out_specs=_row_spec(tm, D_MODEL),
        out_shape=jax.ShapeDtypeStruct((rows, D_MODEL), F32),
        compiler_params=_params(1),
        name="merge",
    )(x, o, y, z, ga, gs, mod, g_ssd, wa, ws, wo)


def _fcum_sample_kernel(past_ref, new_ref, fp_ref, fn_ref):
    p = past_ref.shape[1]
    upper_p = (_iota((p, p), 0) <= _iota((p, p), 1)).astype(BF16)
    upper_n = (_iota((LANES, LANES), 0) <= _iota((LANES, LANES), 1)).astype(BF16)
    past = past_ref[...]
    fp_ref[...] = _dot01_right(past, upper_p)
    total = _dot01_right(past, jnp.ones((p, LANES), BF16))
    fn_ref[...] = total + _dot01_right(new_ref[...], upper_n)


def _fcum_sample(past, new):
    rows, p = past.shape
    return pl.pallas_call(
        _fcum_sample_kernel,
        out_shape=[jax.ShapeDtypeStruct((rows, p), F32), jax.ShapeDtypeStruct((rows, LANES), F32)],
        compiler_params=pltpu.CompilerParams(vmem_limit_bytes=VMEM_LIMIT_BYTES),
        name="fcum_sample",
    )(past, new)


def _pair_normalize(acc_even, acc_odd):
    lane = _iota((1, LANES), 1)
    return jnp.where(lane < HEAD_DIM_A,
                     acc_even / acc_even[:, HEAD_DIM_A:HEAD_DIM_A + 1], acc_odd / acc_odd[:, 0:1])


def _attn_prompt_kernel(qe_ref, qo_ref, ke_ref, ko_ref, ve_ref, vo_ref, o_ref, m_sc, acc_sc):
    qi = pl.program_id(1)
    tq = qe_ref.shape[0]
    tk = ATT_KEY_BLOCK
    blocks_per_q = tq // tk
    q_refs, k_refs, v_refs = (qe_ref, qo_ref), (ke_ref, ko_ref), (ve_ref, vo_ref)
    m_sc[...] = jnp.full(m_sc.shape, -jnp.inf, F32)
    acc_sc[...] = jnp.zeros(acc_sc.shape, F32)

    def unit(r0, nrows, j, key_start):
        off = pl.multiple_of(j * tk, tk)
        rows = slice(r0, r0 + nrows)
        scores = []
        for hh in range(2):
            s = lax.dot_general(q_refs[hh][rows, :], k_refs[hh][pl.ds(off, tk), :], _NT,
                                preferred_element_type=F32)
            if key_start is not None:
                keep = (key_start + _iota((nrows, tk), 1)) <= (r0 + _iota((nrows, tk), 0))
                s = jnp.where(keep, s, -jnp.inf)
            scores.append(s)
        probs = []
        for hh in range(2):
            m_prev = m_sc[hh, rows, :]
            m_new = jnp.maximum(m_prev, jnp.max(scores[hh], axis=1, keepdims=True))
            p = jnp.exp(scores[hh] - jnp.concatenate([m_new] * (tk // LANES), axis=1))
            probs.append((p.astype(BF16), jnp.exp(m_prev - m_new)))
            m_sc[hh, rows, :] = m_new
        for hh in range(2):
            p, alpha = probs[hh]
            pv = jnp.dot(p, v_refs[hh][pl.ds(off, tk), :], preferred_element_type=F32)
            acc_sc[hh, rows, :] = alpha * acc_sc[hh, rows, :] + pv

    def run_blocks(first, count):
        for d in range(count):
            unit(0, tq, first + d, None)

    def body(trip, carry):
        run_blocks(trip * ATT_BLOCKS_PER_TRIP, ATT_BLOCKS_PER_TRIP)
        return carry

    first_diag = qi * blocks_per_q
    full_trips = first_diag // ATT_BLOCKS_PER_TRIP
    lax.fori_loop(0, full_trips, body, 0)
    left = first_diag - full_trips * ATT_BLOCKS_PER_TRIP
    count = ATT_BLOCKS_PER_TRIP // 2
    done = full_trips * ATT_BLOCKS_PER_TRIP
    while count >= blocks_per_q:
        has = (left // count) % 2 == 1

        @pl.when(has)
        def _(done=done, count=count):
            run_blocks(done, count)

        done = done + jnp.where(has, count, 0)
        count //= 2
    for d in range(blocks_per_q):
        unit(d * tk, tq - d * tk, first_diag + d, d * tk)

    o_ref[...] = _pair_normalize(acc_sc[0], acc_sc[1]).astype(o_ref.dtype)


def _attn_prompt(qe, qo, ke, ko, ve, vo):
    s = qe.shape[0]
    tq = ATT_QUERY_BLOCK
    n_pairs = W_ATT // LANES
    q_spec = pl.BlockSpec((tq, LANES), lambda p, i: (i, p))
    kv_spec = pl.BlockSpec((s, LANES), lambda p, i: (0, p))
    return pl.pallas_call(
        _attn_prompt_kernel,
        grid=(n_pairs, s // tq),
        in_specs=[q_spec, q_spec, kv_spec, kv_spec, kv_spec, kv_spec],
        out_specs=q_spec,
        out_shape=jax.ShapeDtypeStruct((s, W_ATT), BF16),
        scratch_shapes=[pltpu.VMEM((2, tq, LANES), F32), pltpu.VMEM((2, tq, LANES), F32)],
        compiler_params=_params(2),
        name="attn_prompt",
    )(qe, qo, ke, ko, ve, vo)


def _attn_sample_kernel(q_ref, kn_ref, vn_ref, kc_ref, vc_ref, fp_ref, fn_ref, o_ref):
    n = q_ref.shape[0]
    keep = _iota((n, n), 1) <= _iota((n, n), 0)
    for h in range(N_HEADS_A):
        cols = slice(h * HEAD_DIM_A, (h + 1) * HEAD_DIM_A)
        q = q_ref[:, cols]
        s_past = jnp.dot(q, kc_ref[h].astype(BF16), preferred_element_type=F32) - fp_ref[h:h + 1, :]
        s_new = lax.dot_general(q, kn_ref[:, cols], _NT, preferred_element_type=F32) - fn_ref[h:h + 1, :n]
        s_new = jnp.where(keep, s_new, -jnp.inf)
        m = jnp.maximum(jnp.max(s_past, axis=1, keepdims=True), jnp.max(s_new, axis=1, keepdims=True))
        p_past = jnp.exp(s_past - m)
        p_new = jnp.exp(s_new - m)
        denom = jnp.sum(p_past, axis=1, keepdims=True) + jnp.sum(p_new, axis=1, keepdims=True)
        acc = (lax.dot_general(p_past.astype(BF16), vc_ref[h].astype(BF16), _NT, preferred_element_type=F32)
               + jnp.dot(p_new.astype(BF16), vn_ref[:, cols], preferred_element_type=F32))
        o_ref[:, cols] = (acc / denom).astype(o_ref.dtype)


def _attn_sample(q, kn, vn, cache_k, cache_v, f_past, f_new, n_batch, n_new):
    past = f_past.shape[2]
    tok = pl.BlockSpec((n_new, W_ATT), lambda b: (b, 0))
    cache = pl.BlockSpec((None, N_HEADS_A, HEAD_DIM_A, past), lambda b: (b, 0, 0, 0))
    return pl.pallas_call(
        _attn_sample_kernel,
        grid=(n_batch,),
        in_specs=[tok, tok, tok, cache, cache,
                  pl.BlockSpec((None, N_HEADS_A, past), lambda b: (b, 0, 0)),
                  pl.BlockSpec((None, N_HEADS_A, LANES), lambda b: (b, 0, 0))],
        out_specs=tok,
        out_shape=jax.ShapeDtypeStruct((n_batch * n_new, W_ATT), BF16),
        compiler_params=_params(1),
        name="attn_sample",
    )(q, kn, vn, cache_k, cache_v, f_past, f_new)


def _ssd_chunk(xs, b_all, c_all, dt, alog_ref, dsk_ref, y_ref, ht_sc):
    t = xs.shape[0]
    hp_group = D_INNER // SSD_GROUPS
    heads_per_group = SSD_HEADS // SSD_GROUPS
    lane = _iota((1, LANES), 1)

    a_row = -jnp.exp(alog_ref[...])
    lower = (_iota((t, t), 1) <= _iota((t, t), 0)).astype(BF16)
    acs = _dot01_left(lower, dt * a_row)
    acs_end = acs[t - 1:t, :]
    compact = jnp.concatenate([jnp.exp(acs_end - acs) * dt, dt, jnp.exp(acs)], axis=0)
    expand = ((_iota((LANES, D_INNER), 1) // SSD_HEADDIM) == _iota((LANES, D_INNER), 0)).astype(BF16)
    wide = _dot01_right(compact, expand, pieces=2)
    xs_state = (xs * wide[0:t]).astype(BF16)
    xs_dt = (xs * wide[t:2 * t]).astype(BF16)
    e_acs = wide[2 * t:3 * t]
    chunk_decay = e_acs[t - 1:t, :]

    eye = (_iota((LANES, LANES), 0) == _iota((LANES, LANES), 1)).astype(BF16)
    acs_t = _transpose01(eye, acs)
    causal = _iota((t, t), 1) <= _iota((t, t), 0)

    for g in range(SSD_GROUPS):
        b_g = b_all[:, g * SSD_STATE:(g + 1) * SSD_STATE]
        c_g = c_all[:, g * SSD_STATE:(g + 1) * SSD_STATE]
        cols = slice(g * hp_group, (g + 1) * hp_group)
        ht_g = ht_sc[:, cols]
        y_off = jnp.dot(c_g, ht_g.astype(BF16), preferred_element_type=F32) * e_acs[:, cols]
        states = lax.dot_general(b_g, xs_state[:, cols], _TN, preferred_element_type=F32)
        ht_sc[:, cols] = ht_g * chunk_decay[:, cols] + states
        cb = lax.dot_general(c_g, b_g, _NT, preferred_element_type=F32)
        for pair in range(heads_per_group // 2):
            h0 = g * heads_per_group + 2 * pair
            lo = h0 * SSD_HEADDIM
            x_pair = xs_dt[:, lo:lo + LANES]
            y_heads = []
            for h in (h0, h0 + 1):
                seg = acs[:, h:h + 1] - acs_t[h:h + 1, :]
                w = (cb * jnp.exp(jnp.where(causal, seg, -jnp.inf))).astype(BF16)
                y_heads.append(jnp.dot(w, x_pair, preferred_element_type=F32))
            y_pair = (y_off[:, lo - g * hp_group:lo - g * hp_group + LANES]
                      + jnp.where(lane < SSD_HEADDIM, y_heads[0], y_heads[1])
                      + dsk_ref[:, lo:lo + LANES] * xs[:, lo:lo + LANES])
            y_ref[:, lo:lo + LANES] = y_pair.astype(y_ref.dtype)


def _load_state_transposed(h0_ref, ht_sc):
    for b in range(D_INNER // LANES):
        ht_sc[:, b * LANES:(b + 1) * LANES] = h0_ref[b * LANES:(b + 1) * LANES, :].T


def _store_state(ht_sc, hout_ref):
    for b in range(D_INNER // LANES):
        hout_ref[b * LANES:(b + 1) * LANES, :] = ht_sc[:, b * LANES:(b + 1) * LANES].T


def _ssd_conv_kernel(xbc_ref, dt_ref, cst_ref, h0_ref, cw_ref, cb_ref, alog_ref, dsk_ref,
                     y_ref, hout_ref, cout_ref, ht_sc, cbuf_sc):
    c = pl.program_id(1)
    t = xbc_ref.shape[0]

    @pl.when(c == 0)
    def _():
        cbuf_sc[CONV_FIRST:CONV_PAD, :] = cst_ref[...]
        _load_state_transposed(h0_ref, ht_sc)

    xc, tail = _causal_conv_silu(cbuf_sc, xbc_ref[...], cw_ref[...], cb_ref[...], slice(None), t)
    _ssd_chunk(xc[:, :D_INNER], xc[:, D_INNER:D_INNER + SSD_GROUPS * SSD_STATE].astype(BF16),
               xc[:, D_INNER + SSD_GROUPS * SSD_STATE:].astype(BF16), dt_ref[...], alog_ref, dsk_ref, y_ref, ht_sc)

    @pl.when(c == pl.num_programs(1) - 1)
    def _():
        cout_ref[...] = tail
        _store_state(ht_sc, hout_ref)


def _ssd_kernel(xs_ref, bc_ref, dt_ref, h0_ref, alog_ref, dsk_ref, y_ref, hout_ref, ht_sc):
    c = pl.program_id(1)

    @pl.when(c == 0)
    def _():
        _load_state_transposed(h0_ref, ht_sc)

    gn = SSD_GROUPS * SSD_STATE
    _ssd_chunk(xs_ref[...].astype(F32), bc_ref[:, :gn], bc_ref[:, gn:], dt_ref[...], alog_ref, dsk_ref,
               y_ref, ht_sc)

    @pl.when(c == pl.num_programs(1) - 1)
    def _():
        _store_state(ht_sc, hout_ref)


def _ssd(x_in, dt, ssm_state, a_log_row, d_skip_row, n_batch, seq, chunk, conv=None):
    assert seq % chunk == 0 and chunk >= CONV_TAIL
    nc = seq // chunk
    rows = lambda b, c: (b * nc + c, 0)
    per_batch3 = lambda b, c: (b, 0, 0)
    const2 = lambda b, c: (0, 0)
    sd = jax.ShapeDtypeStruct
    state_spec = pl.BlockSpec((None, D_INNER, SSD_STATE), per_batch3)
    tail_specs = [pl.BlockSpec((1, LANES), const2), pl.BlockSpec((1, D_INNER), const2)]
    y_spec = pl.BlockSpec((chunk, D_INNER), rows)
    y_shape = sd((n_batch * seq, D_INNER), BF16)
    state_shape = sd((n_batch, D_INNER, SSD_STATE), F32)
    state_scratch = pltpu.VMEM((SSD_STATE, D_INNER), F32)
    if conv is None:
        xs, bc = x_in
        y, h_new = pl.pallas_call(
            _ssd_kernel,
            grid=(n_batch, nc),
            in_specs=[pl.BlockSpec((chunk, D_INNER), rows), pl.BlockSpec((chunk, CONV_DIM - D_INNER), rows),
                      pl.BlockSpec((chunk, LANES), rows), state_spec] + tail_specs,
            out_specs=[y_spec, state_spec],
            out_shape=[y_shape, state_shape],
            scratch_shapes=[state_scratch],
            compiler_params=_params(2),
            name="ssd_scan",
        )(xs, bc, dt, ssm_state, a_log_row, d_skip_row)
        return y, h_new, None
    conv_state, conv_w, conv_b = conv
    conv_spec = pl.BlockSpec((None, CONV_TAIL, CONV_DIM), per_batch3)
    return pl.pallas_call(
        _ssd_conv_kernel,
        grid=(n_batch, nc),
        in_specs=[pl.BlockSpec((chunk, CONV_DIM), rows), pl.BlockSpec((chunk, LANES), rows),
                  conv_spec, state_spec,
                  pl.BlockSpec((CONV_W, CONV_DIM), const2), pl.BlockSpec((1, CONV_DIM), const2)] + tail_specs,
        out_specs=[y_spec, state_spec, conv_spec],
        out_shape=[y_shape, state_shape, sd((n_batch, CONV_TAIL, CONV_DIM), F32)],
        scratch_shapes=[state_scratch, pltpu.VMEM((CONV_PAD + chunk, CONV_DIM), F32)],
        compiler_params=_params(2),
        name="ssd_conv_scan",
    )(x_in, dt, conv_state, ssm_state, conv_w, conv_b, a_log_row, d_skip_row)


def _pad_lanes(row, width=LANES):
    return jnp.pad(row, ((0, 0), (0, width - row.shape[1])))


def _layer_weights(w_in, b_f, dt_bias, a_log, d_skip, prm):
    o = 0
    pieces = {}
    for name, width in (("q", W_ATT), ("k", W_ATT), ("v", W_ATT), ("f", N_HEADS_A), ("z", D_INNER),
                        ("xbc", CONV_DIM), ("dt", SSD_HEADS), ("ga", D_MODEL), ("gs", D_MODEL)):
        pieces[name] = w_in[:, o:o + width]
        o += width
    w = {k: pieces[k].astype(BF16) for k in ("q", "k", "v", "z", "xbc", "ga", "gs")}
    w["small"] = jnp.concatenate([_pad_lanes(pieces["f"]), _pad_lanes(pieces["dt"])], axis=1).astype(BF16)
    w["small_bias"] = jnp.concatenate([_pad_lanes(b_f[None, :]), _pad_lanes(dt_bias[None, :])], axis=1)
    w["a_log_row"] = _pad_lanes(a_log[None, :])
    w["d_skip_row"] = jnp.repeat(d_skip, SSD_HEADDIM)[None, :]
    for k, v in prm.items():
        w[k] = v.astype(BF16) if v.ndim == 2 and v.shape[0] >= D_MODEL else v
    return w


def _mixer_and_ffn(x, mod, w, tm, g_final, attend, single_sequence, conv_state, ssm_state, n_batch, seq, chunk):
    x, _ = _ffn(x, mod, 0, w["g_ffn1"], w["w1_ffn1"], w["w3_ffn1"], w["w2_ffn1"], g_final, tm, False)
    *att_ops, k32, v32, logf, dt = _inproj_a(x, mod, w["g_mix"], w["q"], w["k"], w["v"],
                                             w["small"], w["small_bias"], tm, single_sequence)
    o = attend(att_ops, logf)
    z, ga, gs = _inproj_gates(x, mod, w["g_mix"], w["z"], w["ga"], w["gs"], tm)
    if single_sequence:
        xs, bc, conv_new = _inproj_xbc(x, mod, w["g_mix"], w["xbc"], tm, (conv_state[0], w["conv_w"], w["conv_b"]))
        y, ssm_new, _ = _ssd((xs, bc), dt, ssm_state, w["a_log_row"], w["d_skip_row"], n_batch, seq, chunk)
        conv_new = conv_new[None]
    else:
        (xbc,) = _inproj_xbc(x, mod, w["g_mix"], w["xbc"], tm)
        y, ssm_new, conv_new = _ssd(xbc, dt, ssm_state, w["a_log_row"], w["d_skip_row"], n_batch, seq, chunk,
                                    (conv_state, w["conv_w"], w["conv_b"]))
    x = _merge(x, o, y, z, ga, gs, mod, w["g_ssd"], w["w_a"], w["w_s"], w["w_out"], tm)
    x, y_final = _ffn(x, mod, 6, w["g_ffn2"], w["w1_ffn2"], w["w3_ffn2"], w["w2_ffn2"], g_final, tm, True)
    return x, y_final, (k32, v32, logf, ssm_new, conv_new)


def kernel(x_prompt, x_sample, c_prompt, c_sample, cache_k, cache_v, cache_logf, state_ssm, state_conv,
           w_ada, b_ada, g_ffn1, w1_ffn1, w3_ffn1, w2_ffn1, g_mix, w_in, b_f, conv_w, conv_b,
           dt_bias, a_log, d_skip, g_ssd, w_a, w_s, w_out, g_ffn2, w1_ffn2, w3_ffn2, w2_ffn2, g_final):
    depth = w_ada.shape[0]
    bp, seq_p, _ = x_prompt.shape
    bs, seq_s, _ = x_sample.shape
    past = cache_k.shape[2]
    assert bp == 1

    xp = x_prompt.reshape(bp * seq_p, D_MODEL)
    xs = x_sample.reshape(bs * seq_s, D_MODEL)
    c_rows = bs + bp
    c_all = jnp.pad(jnp.concatenate([c_sample, c_prompt], axis=0), ((0, -c_rows % 8), (0, 0)))
    gf = g_final[None, :]

    outs_p = [[] for _ in range(5)]
    outs_s = [[] for _ in range(5)]
    yp = ys = None
    for l in range(depth):
        prm = {"g_ffn1": g_ffn1[l][None, :], "w1_ffn1": w1_ffn1[l], "w3_ffn1": w3_ffn1[l], "w2_ffn1": w2_ffn1[l],
               "g_mix": g_mix[l][None, :], "conv_w": conv_w[l], "conv_b": conv_b[l][None, :],
               "g_ssd": g_ssd[l][None, :], "w_a": w_a[l], "w_s": w_s[l], "w_out": w_out[l],
               "g_ffn2": g_ffn2[l][None, :], "w1_ffn2": w1_ffn2[l], "w3_ffn2": w3_ffn2[l], "w2_ffn2": w2_ffn2[l]}
        w = _layer_weights(w_in[l], b_f[l], dt_bias[l], a_log[l], d_skip[l], prm)

        mod = _ada(c_all, w_ada[l], b_ada[l])
        mod_s = jnp.repeat(mod[:bs], seq_s, axis=0)
        mod_p = mod[bs:bs + 1]

        def attend_prompt(att_ops, logf):
            return _attn_prompt(*att_ops)

        def attend_sample(att_ops, logf):
            q, kb, vb = att_ops
            past_rows = cache_logf[l].transpose(0, 2, 1).reshape(bs * N_HEADS_A, past)
            new_rows = logf.reshape(bs, seq_s, N_HEADS_A).transpose(0, 2, 1).reshape(bs * N_HEADS_A, seq_s)
            f_past, f_new = _fcum_sample(past_rows, _pad_lanes(new_rows))
            return _attn_sample(q, kb, vb,
                                cache_k[l].transpose(0, 2, 3, 1), cache_v[l].transpose(0, 2, 3, 1),
                                f_past.reshape(bs, N_HEADS_A, past), f_new.reshape(bs, N_HEADS_A, LANES),
                                bs, seq_s)

        zero_conv = jnp.zeros((bp, CONV_W - 1, CONV_DIM), F32)
        zero_ssm = jnp.zeros((bp, D_INNER, SSD_STATE), F32)
        xp, yp, new_p = _mixer_and_ffn(xp, mod_p, w, ROW_TILE, gf, attend_prompt, True, zero_conv, zero_ssm,
                                       bp, seq_p, SSD_CHUNK_PROMPT)
        xs, ys, new_s = _mixer_and_ffn(xs, mod_s, w, bs * seq_s, gf, attend_sample, False, state_conv[l],
                                       state_ssm[l].reshape(bs, D_INNER, SSD_STATE), bs, seq_s, seq_s)
        for dst, new, nb, sq in ((outs_p, new_p, bp, seq_p), (outs_s, new_s, bs, seq_s)):
            k32, v32, logf, ssm_new, conv_new = new
            dst[0].append(k32.reshape(nb, sq, N_HEADS_A, HEAD_DIM_A))
            dst[1].append(v32.reshape(nb, sq, N_HEADS_A, HEAD_DIM_A))
            dst[2].append(logf.reshape(nb, sq, N_HEADS_A))
            dst[3].append(ssm_new.reshape(nb, SSD_HEADS, SSD_HEADDIM, SSD_STATE))
            dst[4].append(conv_new)

    y_prompt = yp.reshape(bp, seq_p, D_MODEL)
    y_sample = ys.reshape(bs, seq_s, D_MODEL)
    return (y_prompt, y_sample, *[jnp.stack(o) for o in outs_p], *[jnp.stack(o) for o in outs_s])
```
